```python
import jax, jax.numpy as jnp
from jax import lax
import numpy as np

D_MODEL = 2048
BATCH = 1
SEQ = 8192
DEPTH = 4
DEC_BATCH = 2
DEC_SEQ = 16384
PAST_LEN = 128

N_MIXERS = 2
CHUNK = 128
A_HIDDEN = 2 * D_MODEL
A_HALF = A_HIDDEN // 2
A_GROUPS = 16
A_GDIM = A_HALF // A_GROUPS
RNN_WIDTH = D_MODEL
RNN_BLOCKS = 16
RNN_BDIM = RNN_WIDTH // RNN_BLOCKS
CONV_WIDTH = 4
CONV_LEFT = 1
LRU_C = 8.0
N_GATES = 4
N_EXPERTS = 16
N_GROUPS = 4
EXPERTS_PER_GROUP = N_EXPERTS // N_GROUPS
TOP_K = 2
D_EXPERT = D_MODEL // 2
N_A_LAYERS = (DEPTH + 1) // 2
N_B_LAYERS = DEPTH // 2
DEEPNORM_ALPHA = (2.0 * DEPTH) ** 0.25
DEEPNORM_BETA = (8.0 * DEPTH) ** -0.25
LN_EPS = 1e-5

kernel_name = "hybrid_gmlp_rglru_grouped_moe_encoder"


def layer_norm(x, g, b):
    xf = x.astype(jnp.float32)
    mu = jnp.mean(xf, axis=-1, keepdims=True)
    xc = xf - mu
    var = jnp.mean(xc * xc, axis=-1, keepdims=True)
    y = xc * lax.rsqrt(var + LN_EPS) * g.astype(jnp.float32) + b.astype(jnp.float32)
    return y.astype(x.dtype)


def mixer_a(x, w_in, vn_g, vn_b, w_s, b_s, w_out):
    B, S, _ = x.shape
    n = S // CHUNK
    z = jax.nn.gelu(x @ w_in)
    u, v = z[..., :A_HALF], z[..., A_HALF:]
    v = layer_norm(v, vn_g, vn_b).reshape(B, n, CHUNK, A_GROUPS, A_GDIM)
    s = jnp.einsum('bncgd,gmc->bnmgd', v, w_s) + b_s.T[None, None, :, :, None]
    y = u * s.reshape(B, S, A_HALF)
    return y @ w_out


def _lin_combine(c1, c2):
    a1, b1 = c1
    a2, b2 = c2
    return a1 * a2, a2 * b1 + b2


def mixer_b(x, w_in, conv_w, conv_b, w_gates, b_gates, lam, w_out):
    B, S, _ = x.shape
    z = x @ w_in
    gate, xr = z[..., :RNN_WIDTH], z[..., RNN_WIDTH:]
    xp = jnp.pad(xr, ((0, 0), (CONV_LEFT, CONV_WIDTH - 1 - CONV_LEFT), (0, 0)))
    xc = conv_b + sum(conv_w[k] * xp[:, k:k + S] for k in range(CONV_WIDTH))
    xh = xc.reshape(B, S, RNN_BLOCKS, RNN_BDIM)
    g = jnp.einsum('bshi,khij->kbshj', xh, w_gates).reshape(N_GATES, B, S, RNN_WIDTH)
    g = jax.nn.sigmoid(g.astype(jnp.float32) + b_gates.astype(jnp.float32)[:, None, None, :])
    xcf = xc.astype(jnp.float32)
    h = jnp.zeros_like(xcf)
    for d, rev in enumerate((False, True)):
        r, i = g[2 * d], g[2 * d + 1]
        log_a = -LRU_C * r * jax.nn.softplus(-lam[d].astype(jnp.float32))
        a = jnp.exp(log_a)
        bt = jnp.sqrt(-jnp.expm1(2.0 * log_a)) * (i * xcf)
        _, hd = lax.associative_scan(_lin_combine, (a, bt), axis=1, reverse=rev)
        h = h + hd
    y = h.astype(x.dtype) * jax.nn.gelu(gate)
    return y @ w_out


def moe(x, router_w, router_b, w_gu, w_down):
    B, S, D = x.shape
    xf = x.reshape(-1, D)
    T = xf.shape[0]
    logits = (xf @ router_w).astype(jnp.float32) + router_b.astype(jnp.float32)
    probs = jax.nn.softmax(logits, axis=-1)
    pg = probs.reshape(T, N_GROUPS, EXPERTS_PER_GROUP)
    vals, idx = lax.top_k(pg, TOP_K)
    gsel = jnp.argmax(jnp.sum(vals, axis=-1), axis=-1).astype(jnp.int32)
    sel_vals = jnp.take_along_axis(vals, gsel[:, None, None], axis=1)[:, 0]
    sel_idx = jnp.take_along_axis(idx, gsel[:, None, None], axis=1)[:, 0]
    experts = (gsel[:, None] * EXPERTS_PER_GROUP + sel_idx).astype(jnp.int32)
    weights = sel_vals / jnp.sum(sel_vals, axis=-1, keepdims=True)
    e_flat = experts.reshape(-1)
    order = jnp.argsort(e_flat)
    tok = order // TOP_K
    xs = xf[tok]
    sizes = jnp.bincount(e_flat, length=N_EXPERTS).astype(jnp.int32)
    hgu = lax.ragged_dot(xs, w_gu, sizes)
    hh = jax.nn.silu(hgu[:, :D_EXPERT]) * hgu[:, D_EXPERT:]
    ys = lax.ragged_dot(hh, w_down, sizes)
    ys = ys * weights.reshape(-1)[order][:, None].astype(ys.dtype)
    out = jnp.zeros_like(xf).at[tok].add(ys)
    return out.reshape(B, S, D)


def trunk(x, ln_g, ln_b, a_w_in, a_vn_g, a_vn_b, a_w_s, a_b_s, a_w_out,
          b_w_in, b_conv_w, b_conv_b, b_w_gates, b_b_gates, b_lambda, b_w_out,
          router_w, router_b, moe_w_gu, moe_w_down):
    ia = 0
    ib = 0
    for layer in range(DEPTH):
        if layer % N_MIXERS == 0:
            m = mixer_a(x, a_w_in[ia], a_vn_g[ia], a_vn_b[ia], a_w_s[ia], a_b_s[ia], a_w_out[ia])
            ia += 1
        else:
            m = mixer_b(x, b_w_in[ib], b_conv_w[ib], b_conv_b[ib], b_w_gates[ib], b_b_gates[ib],
                        b_lambda[ib], b_w_out[ib])
            ib += 1
        x = layer_norm(DEEPNORM_ALPHA * x + m, ln_g[layer, 0], ln_b[layer, 0])
        f = moe(x, router_w, router_b, moe_w_gu[layer], moe_w_down[layer])
        x = layer_norm(DEEPNORM_ALPHA * x + f, ln_g[layer, 1], ln_b[layer, 1])
    return x


def setup_inputs(seed: int = 0) -> dict:
    key = jax.random.key(seed)
    ks = jax.random.split(key, 24)
    f32 = jnp.float32
    nrm = lambda k, shape, s: jax.random.normal(k, shape, f32) * s
    a0 = jax.random.uniform(ks[13], (N_B_LAYERS, 2, RNN_WIDTH), f32, 0.9, 0.999)
    a_base = a0 ** (1.0 / LRU_C)
    lam = jnp.log(a_base) - jnp.log1p(-a_base)
    return {
        "x_prompt": nrm(ks[0], (BATCH, SEQ, D_MODEL), 1.0),
        "x_sample": nrm(ks[1], (DEC_BATCH, DEC_SEQ, D_MODEL), 1.0),
        "ln_g": 1.0 + nrm(ks[2], (DEPTH, 2, D_MODEL), 0.02),
        "ln_b": nrm(ks[3], (DEPTH, 2, D_MODEL), 0.02),
        "a_w_in": nrm(ks[4], (N_A_LAYERS, D_MODEL, A_HIDDEN), D_MODEL ** -0.5),
        "a_vn_g": 1.0 + nrm(ks[5], (N_A_LAYERS, A_HALF), 0.02),
        "a_vn_b": nrm(ks[6], (N_A_LAYERS, A_HALF), 0.02),
        "a_w_s": nrm(ks[7], (N_A_LAYERS, A_GROUPS, CHUNK, CHUNK), 0.5 * CHUNK ** -0.5),
        "a_b_s": 1.0 + nrm(ks[8], (N_A_LAYERS, A_GROUPS, CHUNK), 0.1),
        "a_w_out": nrm(ks[9], (N_A_LAYERS, A_HALF, D_MODEL), A_HALF ** -0.5 * DEEPNORM_BETA),
        "b_w_in": nrm(ks[10], (N_B_LAYERS, D_MODEL, 2 * RNN_WIDTH), D_MODEL ** -0.5),
        "b_conv_w": nrm(ks[11], (N_B_LAYERS, CONV_WIDTH, RNN_WIDTH), CONV_WIDTH ** -0.5),
        "b_conv_b": nrm(ks[12], (N_B_LAYERS, RNN_WIDTH), 0.02),
        "b_w_gates": nrm(ks[14], (N_B_LAYERS, N_GATES, RNN_BLOCKS, RNN_BDIM, RNN_BDIM), RNN_BDIM ** -0.5),
        "b_b_gates": nrm(ks[15], (N_B_LAYERS, N_GATES, RNN_WIDTH), 0.1),
        "b_lambda": lam,
        "b_w_out": nrm(ks[16], (N_B_LAYERS, RNN_WIDTH, D_MODEL), RNN_WIDTH ** -0.5 * DEEPNORM_BETA),
        "router_w": nrm(ks[17], (D_MODEL, N_EXPERTS), D_MODEL ** -0.5),
        "router_b": nrm(ks[18], (N_EXPERTS,), 0.01),
        "moe_w_gu": nrm(ks[19], (DEPTH, N_EXPERTS, D_MODEL, 2 * D_EXPERT), D_MODEL ** -0.5),
        "moe_w_down": nrm(ks[20], (DEPTH, N_EXPERTS, D_EXPERT, D_MODEL), D_EXPERT ** -0.5 * DEEPNORM_BETA),
    }


def reference(x_prompt, x_sample, ln_g, ln_b, a_w_in, a_vn_g, a_vn_b, a_w_s, a_b_s, a_w_out,
              b_w_in, b_conv_w, b_conv_b, b_w_gates, b_b_gates, b_lambda, b_w_out,
              router_w, router_b, moe_w_gu, moe_w_down):
    y_prompt = trunk(x_prompt, ln_g, ln_b, a_w_in, a_vn_g, a_vn_b, a_w_s, a_b_s, a_w_out,
                     b_w_in, b_conv_w, b_conv_b, b_w_gates, b_b_gates, b_lambda, b_w_out,
                     router_w, router_b, moe_w_gu, moe_w_down)
    y_sample = trunk(x_sample, ln_g, ln_b, a_w_in, a_vn_g, a_vn_b, a_w_s, a_b_s, a_w_out,
                     b_w_in, b_conv_w, b_conv_b, b_w_gates, b_b_gates, b_lambda, b_w_out,
                     router_w, router_b, moe_w_gu, moe_w_down)
    return (y_prompt, y_sample)
```

```python
import functools

import jax
import jax.numpy as jnp
from jax import lax
from jax.experimental import pallas as pl
from jax.experimental.pallas import tpu as pltpu

F32 = jnp.float32
BF16 = jnp.bfloat16
I32 = jnp.int32

D_MODEL = 2048
DEPTH = 4
CHUNK = 128
A_HALF = D_MODEL
A_GROUPS = 16
RNN_WIDTH = D_MODEL
RNN_BLOCKS = 16
RNN_BDIM = RNN_WIDTH // RNN_BLOCKS
LRU_C = 8.0
N_EXPERTS = 16
N_GROUPS = 4
EXPERTS_PER_GROUP = N_EXPERTS // N_GROUPS
D_EXPERT = D_MODEL // 2
DEEPNORM_ALPHA = (2.0 * DEPTH) ** 0.25
LN_EPS = 1e-5

LANES = 128
SUBLANES = 8
VMEM_LIMIT = 56 * 1024 * 1024

TM = 512
TME = 512
TT = 256


def _cparams(sem):
    return pltpu.CompilerParams(dimension_semantics=sem, vmem_limit_bytes=VMEM_LIMIT)


def _resident(shape):
    nd = len(shape)
    return pl.BlockSpec(shape, lambda *_: (0,) * nd, pipeline_mode=pl.Buffered(1))


def _dot(a, b):
    return jnp.dot(a, b, preferred_element_type=F32)


def _ln(x, g, b):
    mu = jnp.mean(x, axis=-1, keepdims=True)
    xc = x - mu
    var = jnp.mean(xc * xc, axis=-1, keepdims=True)
    return xc * lax.rsqrt(var + LN_EPS) * g + b


def _sigmoid(x):
    return 0.5 * jnp.tanh(0.5 * x) + 0.5


def _a1_body(x_ref, w_ref, vg_ref, vb_ref, ws_ref, bs_ref, y_ref, s_scr, *, tm):
    xb = x_ref[...].astype(BF16)
    v = jax.nn.gelu(_dot(xb, w_ref[:, A_HALF:]))
    v = _ln(v, vg_ref[...], vb_ref[...]).astype(BF16)
    for c in range(tm // CHUNK):
        rows = slice(c * CHUNK, (c + 1) * CHUNK)
        for g in range(A_GROUPS):
            cols = slice(g * LANES, (g + 1) * LANES)
            s_scr[rows, cols] = _dot(ws_ref[g], v[rows, cols]) + bs_ref[:, cols]
    u = jax.nn.gelu(_dot(xb, w_ref[:, :A_HALF]))
    y_ref[...] = (u * s_scr[...]).astype(BF16)


def _a1(x, w_in, vn_g, vn_b, w_s, bs_full, tm):
    t = x.shape[0]
    return pl.pallas_call(
        functools.partial(_a1_body, tm=tm),
        grid=(t // tm,),
        in_specs=[
            pl.BlockSpec((tm, D_MODEL), lambda i: (i, 0)),
            _resident((D_MODEL, 2 * A_HALF)),
            _resident((1, A_HALF)),
            _resident((1, A_HALF)),
            _resident((A_GROUPS, CHUNK, CHUNK)),
            _resident((CHUNK, A_HALF)),
        ],
        out_specs=pl.BlockSpec((tm, A_HALF), lambda i: (i, 0)),
        out_shape=jax.ShapeDtypeStruct((t, A_HALF), BF16),
        scratch_shapes=[pltpu.VMEM((tm, A_HALF), F32)],
        compiler_params=_cparams(("arbitrary",)),
        name="a1_gmlp_front",
    )(x, w_in, vn_g, vn_b, w_s, bs_full)


def _op_a_body(y_ref, w_ref, x_ref, g_ref, b_ref, o_ref):
    m = _dot(y_ref[...], w_ref[...])
    o_ref[...] = _ln(DEEPNORM_ALPHA * x_ref[...] + m, g_ref[...], b_ref[...])


def _op_b_body(hf_ref, hb_ref, gg_ref, w_ref, x_ref, g_ref, b_ref, o_ref):
    y = ((hf_ref[...] + hb_ref[...]) * gg_ref[...].astype(F32)).astype(BF16)
    m = _dot(y, w_ref[...])
    o_ref[...] = _ln(DEEPNORM_ALPHA * x_ref[...] + m, g_ref[...], b_ref[...])


def _out_proj(body, acts, w_out, x, g, b, tm, name):
    t = x.shape[0]
    tile = pl.BlockSpec((tm, D_MODEL), lambda i: (i, 0))
    return pl.pallas_call(
        body,
        grid=(t // tm,),
        in_specs=[tile] * len(acts) + [_resident(w_out.shape), tile, _resident((1, D_MODEL)), _resident((1, D_MODEL))],
        out_specs=tile,
        out_shape=jax.ShapeDtypeStruct((t, D_MODEL), F32),
        compiler_params=_cparams(("arbitrary",)),
        name=name,
    )(*acts, w_out, x, g, b)


def _b1_body(x_ref, w_ref, gg_ref, xr_ref):
    xb = x_ref[...].astype(BF16)
    gg_ref[...] = jax.nn.gelu(_dot(xb, w_ref[:, :RNN_WIDTH])).astype(BF16)
    xr_ref[...] = _dot(xb, w_ref[:, RNN_WIDTH:])


def _b1(x, w_in, tm):
    t = x.shape[0]
    tile = pl.BlockSpec((tm, D_MODEL), lambda i: (i, 0))
    return pl.pallas_call(
        _b1_body,
        grid=(t // tm,),
        in_specs=[tile, _resident((D_MODEL, 2 * RNN_WIDTH))],
        out_specs=[tile, tile],
        out_shape=[jax.ShapeDtypeStruct((t, RNN_WIDTH), BF16), jax.ShapeDtypeStruct((t, RNN_WIDTH), F32)],
        compiler_params=_cparams(("arbitrary",)),
        name="b1_rglru_front",
    )(x, w_in)


def _scan_rows(a_s, b_s, o_ref, carry_scr, *, tt, reverse):
    n_grp = tt // SUBLANES
    width = 512
    row_id = lax.broadcasted_iota(I32, (SUBLANES, width), 0)

    def step(k, carry):
        g = (n_grp - 1 - k) if reverse else k
        r0 = pl.multiple_of(g * SUBLANES, SUBLANES)
        new = []
        for q in range(RNN_WIDTH // width):
            cols = slice(q * width, (q + 1) * width)
            a = a_s[pl.ds(r0, SUBLANES), cols]
            b = b_s[pl.ds(r0, SUBLANES), cols]
            for d in (1, 2, 4):
                if reverse:
                    keep = row_id < SUBLANES - d
                    sh = SUBLANES - d
                else:
                    keep = row_id >= d
                    sh = d
                a_sh = jnp.where(keep, pltpu.roll(a, sh, axis=0), 1.0)
                b_sh = jnp.where(keep, pltpu.roll(b, sh, axis=0), 0.0)
                b = a * b_sh + b
                a = a * a_sh
            h = a * carry[q] + b
            o_ref[pl.ds(r0, SUBLANES), cols] = h
            edge = h[0:1, :] if reverse else h[SUBLANES - 1:SUBLANES, :]
            new.append(jnp.broadcast_to(edge, (SUBLANES, width)))
        return tuple(new)

    init = tuple(carry_scr[:, q * width:(q + 1) * width] for q in range(RNN_WIDTH // width))
    fin = lax.fori_loop(0, n_grp, step, init)
    for q in range(RNN_WIDTH // width):
        carry_scr[:, q * width:(q + 1) * width] = fin[q]


def _b2_direction(d, x_ref, prev_ref, next_ref, at_start, at_end, reset, cw_ref, cb_ref, wg_ref, bg_ref, sp_ref,
                  o_ref, carry_scr, win, xc_s, a_s, b_s, *, tt):
    win[0:SUBLANES, :] = jnp.where(at_start, 0.0, prev_ref[...])
    win[SUBLANES:SUBLANES + tt, :] = x_ref[...]
    win[SUBLANES + tt:2 * SUBLANES + tt, :] = jnp.where(at_end, 0.0, next_ref[...])
    xc = cb_ref[...]
    for k in range(4):
        xc = xc + cw_ref[k:k + 1, :] * win[SUBLANES - 1 + k:SUBLANES - 1 + k + tt, :]
    xc_s[...] = xc
    for h in range(RNN_BLOCKS):
        cols = slice(h * RNN_BDIM, (h + 1) * RNN_BDIM)
        xh = xc_s[:, cols]
        g2 = _dot(xh.astype(BF16), wg_ref[d, h])
        r = _sigmoid(g2[:, :RNN_BDIM] + bg_ref[2 * d:2 * d + 1, cols])
        ig = _sigmoid(g2[:, RNN_BDIM:] + bg_ref[2 * d + 1:2 * d + 2, cols])
        log_a = (-LRU_C * sp_ref[d:d + 1, cols]) * r
        a = jnp.exp(log_a)
        mult = jnp.sqrt(-jnp.tanh(log_a) * (a * a + 1.0))
        a_s[:, cols] = a
        b_s[:, cols] = mult * (ig * xh)

    @pl.when(reset)
    def _():
        carry_scr[...] = jnp.zeros_like(carry_scr)

    _scan_rows(a_s, b_s, o_ref, carry_scr, tt=tt, reverse=(d == 1))


def _b2_body(xf_ref, pf_ref, nf_ref, xb_ref, pb_ref, nb_ref, cw_ref, cb_ref, wg_ref, bg_ref, lam_ref,
             hf_ref, hb_ref, cf_scr, cbk_scr, sp_scr, win, xc_s, a_s, b_s, *, tt, n_t, starts, ends):
    i = pl.program_id(0)
    j = n_t - 1 - i

    def hits(idx, marks):
        r = idx == marks[0]
        for m in marks[1:]:
            r = jnp.logical_or(r, idx == m)
        return r

    sp_scr[...] = jax.nn.softplus(-lam_ref[...])
    common = dict(cw_ref=cw_ref, cb_ref=cb_ref, wg_ref=wg_ref, bg_ref=bg_ref, sp_ref=sp_scr,
                  win=win, xc_s=xc_s, a_s=a_s, b_s=b_s, tt=tt)
    f_start, f_end = hits(i, starts), hits(i, ends)
    _b2_direction(0, xf_ref, pf_ref, nf_ref, f_start, f_end, f_start, o_ref=hf_ref, carry_scr=cf_scr, **common)
    b_start, b_end = hits(j, starts), hits(j, ends)
    _b2_direction(1, xb_ref, pb_ref, nb_ref, b_start, b_end, b_end, o_ref=hb_ref, carry_scr=cbk_scr, **common)


def _b2(xr, conv_w, conv_b, wg, b_gates, lam, seq_lens, tt):
    t = xr.shape[0]
    n_t = t // tt
    hp = tt // SUBLANES
    n_h = t // SUBLANES
    bounds = [0]
    for s in seq_lens:
        bounds.append(bounds[-1] + s)
    starts = tuple(b // tt for b in bounds[:-1])
    ends = tuple(b // tt - 1 for b in bounds[1:])
    w = RNN_WIDTH
    main_f = pl.BlockSpec((tt, w), lambda i: (i, 0))
    prev_f = pl.BlockSpec((SUBLANES, w), lambda i: (jnp.maximum(i * hp - 1, 0), 0))
    next_f = pl.BlockSpec((SUBLANES, w), lambda i: (jnp.minimum((i + 1) * hp, n_h - 1), 0))
    main_b = pl.BlockSpec((tt, w), lambda i: (n_t - 1 - i, 0))
    prev_b = pl.BlockSpec((SUBLANES, w), lambda i: (jnp.maximum((n_t - 1 - i) * hp - 1, 0), 0))
    next_b = pl.BlockSpec((SUBLANES, w), lambda i: (jnp.minimum((n_t - i) * hp, n_h - 1), 0))
    return pl.pallas_call(
        functools.partial(_b2_body, tt=tt, n_t=n_t, starts=starts, ends=ends),
        grid=(n_t,),
        in_specs=[main_f, prev_f, next_f, main_b, prev_b, next_b,
                  _resident((4, w)), _resident((1, w)), _resident(wg.shape), _resident((4, w)), _resident((2, w))],
        out_specs=[main_f, main_b],
        out_shape=[jax.ShapeDtypeStruct((t, w), F32), jax.ShapeDtypeStruct((t, w), F32)],
        scratch_shapes=[pltpu.VMEM((SUBLANES, w), F32), pltpu.VMEM((SUBLANES, w), F32), pltpu.VMEM((2, w), F32),
                        pltpu.VMEM((tt + 2 * SUBLANES, w), F32), pltpu.VMEM((tt, w), F32),
                        pltpu.VMEM((tt, w), F32), pltpu.VMEM((tt, w), F32)],
        compiler_params=_cparams(("arbitrary",)),
        name="b2_conv_gates_scan",
    )(xr, xr, xr, xr, xr, xr, conv_w, conv_b, wg, b_gates, lam)


def _top2_of4(a):
    m1 = jnp.maximum(jnp.maximum(a[0], a[1]), jnp.maximum(a[2], a[3]))
    i1 = jnp.where(a[0] == m1, 0, jnp.where(a[1] == m1, 1, jnp.where(a[2] == m1, 2, 3)))
    b = [jnp.where(i1 == k, -1.0, a[k]) for k in range(4)]
    m2 = jnp.maximum(jnp.maximum(b[0], b[1]), jnp.maximum(b[2], b[3]))
    i2 = jnp.where(b[0] == m2, 0, jnp.where(b[1] == m2, 1, jnp.where(b[2] == m2, 2, 3)))
    return m1, i1, m2, i2


def _router_body(x_ref, wh_ref, wl_ref, rb_ref, tri_ref, ert_ref, wb0_ref, wb1_ref, cnt_ref, base_scr, *, tm):
    @pl.when(pl.program_id(0) == 0)
    def _():
        base_scr[...] = jnp.zeros_like(base_scr)

    x = x_ref[...]
    xh = x.astype(BF16)
    xl = (x - xh.astype(F32)).astype(BF16)
    wh = wh_ref[...]
    logits = _dot(xh, wh) + _dot(xl, wh) + _dot(xh, wl_ref[...]) + rb_ref[...]
    lt = jnp.transpose(logits)
    l = [lt[e:e + 1, :] for e in range(N_EXPERTS)]
    mx = l[0]
    for e in range(1, N_EXPERTS):
        mx = jnp.maximum(mx, l[e])
    ex = [jnp.exp(v - mx) for v in l]
    den = ex[0]
    for e in range(1, N_EXPERTS):
        den = den + ex[e]
    p = [v / den for v in ex]

    tops = [_top2_of4(p[g * EXPERTS_PER_GROUP:(g + 1) * EXPERTS_PER_GROUP]) for g in range(N_GROUPS)]
    score = [t[0] + t[2] for t in tops]
    best = jnp.maximum(jnp.maximum(score[0], score[1]), jnp.maximum(score[2], score[3]))
    gsel = jnp.where(score[0] == best, 0, jnp.where(score[1] == best, 1, jnp.where(score[2] == best, 2, 3)))

    def pick(field):
        return jnp.where(gsel == 0, tops[0][field],
                         jnp.where(gsel == 1, tops[1][field], jnp.where(gsel == 2, tops[2][field], tops[3][field])))

    v1, i1, v2, i2 = pick(0), pick(1), pick(2), pick(3)
    e1 = gsel * EXPERTS_PER_GROUP + i1
    e2 = gsel * EXPERTS_PER_GROUP + i2
    vs = v1 + v2
    w1 = v1 / vs
    w2 = v2 / vs

    eid = lax.broadcasted_iota(I32, (N_EXPERTS, tm), 0)
    hit1 = eid == e1
    hit2 = eid == e2
    oh = jnp.where(hit1, 1.0, 0.0) + jnp.where(hit2, 1.0, 0.0)
    before = _dot(oh.astype(BF16), tri_ref[...])
    tot = before + jnp.concatenate([base_scr[...]] * (tm // LANES), axis=1)
    r1 = jnp.sum(jnp.where(hit1, tot, 0.0), axis=0, keepdims=True)
    r2 = jnp.sum(jnp.where(hit2, tot, 0.0), axis=0, keepdims=True)
    base_scr[...] = base_scr[...] + jnp.sum(oh, axis=1, keepdims=True)
    cnt_ref[...] = base_scr[...]

    zero = jnp.zeros((1, tm), I32)
    ert_ref[...] = jnp.concatenate([e1, e2, r1.astype(I32), r2.astype(I32), zero, zero, zero, zero], axis=0)
    wb0_ref[...] = jnp.transpose(jnp.broadcast_to(w1, (LANES, tm)))
    wb1_ref[...] = jnp.transpose(jnp.broadcast_to(w2, (LANES, tm)))


def _router(x, rwh, rwl, rb, tri, tm):
    t = x.shape[0]
    return pl.pallas_call(
        functools.partial(_router_body, tm=tm),
        grid=(t // tm,),
        in_specs=[pl.BlockSpec((tm, D_MODEL), lambda i: (i, 0)),
                  _resident((D_MODEL, LANES)), _resident((D_MODEL, LANES)), _resident((1, LANES)), _resident((tm, tm))],
        out_specs=[pl.BlockSpec((SUBLANES, tm), lambda i: (0, i)),
                   pl.BlockSpec((tm, LANES), lambda i: (i, 0)),
                   pl.BlockSpec((tm, LANES), lambda i: (i, 0)),
                   pl.BlockSpec((N_EXPERTS, LANES), lambda i: (0, 0))],
        out_shape=[jax.ShapeDtypeStruct((SUBLANES, t), I32),
                   jax.ShapeDtypeStruct((t, LANES), F32),
                   jax.ShapeDtypeStruct((t, LANES), F32),
                   jax.ShapeDtypeStruct((N_EXPERTS, LANES), F32)],
        scratch_shapes=[pltpu.VMEM((N_EXPERTS, LANES), F32)],
        compiler_params=_cparams(("arbitrary",)),
        name="moe_router",
    )(x, rwh, rwl, rb, tri)


def _row_copy(src, s, dst, d, sem):
    return pltpu.make_async_copy(src.at[pl.ds(s, 1), :], dst.at[pl.ds(d, 1), :], sem)


def _dispatch_body(off_ref, ert_ref, x_hbm, xs_in, xs_out, sem, *, tm):
    del xs_in
    base = pl.program_id(0) * tm

    def issue(t, c):
        for k in range(2):
            p = off_ref[ert_ref[k, t]] + ert_ref[2 + k, t]
            _row_copy(x_hbm, base + t, xs_out, p, sem).start()
        return c

    lax.fori_loop(0, tm, issue, 0)

    def drain(t, c):
        for k in range(2):
            _row_copy(x_hbm, 0, xs_out, 0, sem).wait()
        return c

    lax.fori_loop(0, tm, drain, 0)


def _dispatch(off, ert, x, xs_init, tm):
    t = x.shape[0]
    return pl.pallas_call(
        functools.partial(_dispatch_body, tm=tm),
        grid_spec=pltpu.PrefetchScalarGridSpec(
            num_scalar_prefetch=1,
            grid=(t // tm,),
            in_specs=[pl.BlockSpec((SUBLANES, tm), lambda i, o: (0, i), memory_space=pltpu.SMEM),
                      pl.BlockSpec(memory_space=pl.ANY), pl.BlockSpec(memory_space=pl.ANY)],
            out_specs=pl.BlockSpec(memory_space=pl.ANY),
            scratch_shapes=[pltpu.SemaphoreType.DMA(())]),
        out_shape=jax.ShapeDtypeStruct(xs_init.shape, F32),
        input_output_aliases={3: 0},
        compiler_params=pltpu.CompilerParams(dimension_semantics=("arbitrary",), has_side_effects=True),
        name="moe_dispatch",
    )(off, ert, x, xs_init)


def _ffn_body(te_ref, nu_ref, xs_ref, wgu_ref, wdn_ref, ys_ref):
    used = pl.program_id(0) < nu_ref[0]

    @pl.when(used)
    def _():
        h = _dot(xs_ref[...].astype(BF16), wgu_ref[0])
        hh = (jax.nn.silu(h[:, :D_EXPERT]) * h[:, D_EXPERT:]).astype(BF16)
        ys_ref[...] = _dot(hh, wdn_ref[0])

    @pl.when(jnp.logical_not(used))
    def _():
        ys_ref[...] = jnp.zeros_like(ys_ref)


def _ffn(tile_expert, n_used, xs, w_gu, w_down, tm):
    p = xs.shape[0]
    return pl.pallas_call(
        _ffn_body,
        grid_spec=pltpu.PrefetchScalarGridSpec(
            num_scalar_prefetch=2,
            grid=(p // tm,),
            in_specs=[pl.BlockSpec((tm, D_MODEL), lambda i, te, nu: (i, 0)),
                      pl.BlockSpec((1, D_MODEL, 2 * D_EXPERT), lambda i, te, nu: (te[i], 0, 0)),
                      pl.BlockSpec((1, D_EXPERT, D_MODEL), lambda i, te, nu: (te[i], 0, 0))],
            out_specs=pl.BlockSpec((tm, D_MODEL), lambda i, te, nu: (i, 0))),
        out_shape=jax.ShapeDtypeStruct((p, D_MODEL), F32),
        compiler_params=_cparams(("arbitrary",)),
        name="moe_expert_ffn",
    )(tile_expert, n_used, xs, w_gu, w_down)


def _combine_body(off_ref, ert_ref, ys_hbm, wb0_ref, wb1_ref, x_ref, g_ref, b_ref, o_ref, buf0, buf1, sem, *, tm):
    bufs = (buf0, buf1)

    def issue(t, c):
        for k in range(2):
            p = off_ref[ert_ref[k, t]] + ert_ref[2 + k, t]
            _row_copy(ys_hbm, p, bufs[k], t, sem).start()
        return c

    lax.fori_loop(0, tm, issue, 0)

    def drain(t, c):
        for k in range(2):
            _row_copy(ys_hbm, 0, bufs[k], 0, sem).wait()
        return c

    lax.fori_loop(0, tm, drain, 0)

    reps = D_MODEL // LANES
    w0 = jnp.concatenate([wb0_ref[...]] * reps, axis=1)
    w1 = jnp.concatenate([wb1_ref[...]] * reps, axis=1)
    f = w0 * buf0[...] + w1 * buf1[...]
    o_ref[...] = _ln(DEEPNORM_ALPHA * x_ref[...] + f, g_ref[...], b_ref[...])


def _combine(off, ert, ys, wb0, wb1, x, g, b, tm):
    t = x.shape[0]
    tile = pl.BlockSpec((tm, D_MODEL), lambda i, o: (i, 0))
    wtile = pl.BlockSpec((tm, LANES), lambda i, o: (i, 0))
    vec = pl.BlockSpec((1, D_MODEL), lambda i, o: (0, 0))
    return pl.pallas_call(
        functools.partial(_combine_body, tm=tm),
        grid_spec=pltpu.PrefetchScalarGridSpec(
            num_scalar_prefetch=1,
            grid=(t // tm,),
            in_specs=[pl.BlockSpec((SUBLANES, tm), lambda i, o: (0, i), memory_space=pltpu.SMEM),
                      pl.BlockSpec(memory_space=pl.ANY), wtile, wtile, tile, vec, vec],
            out_specs=tile,
            scratch_shapes=[pltpu.VMEM((tm, D_MODEL), F32), pltpu.VMEM((tm, D_MODEL), F32),
                            pltpu.SemaphoreType.DMA(())]),
        out_shape=jax.ShapeDtypeStruct((t, D_MODEL), F32),
        compiler_params=_cparams(("arbitrary",)),
        name="moe_combine",
    )(off, ert, ys, wb0, wb1, x, g, b)


def _moe_layer(x, rwh, rwl, rb, tri, w_gu, w_down, g, b, tm, tme):
    t = x.shape[0]
    n_tiles = (2 * t) // tme + N_EXPERTS
    ert, wb0, wb1, cnt = _router(x, rwh, rwl, rb, tri, tm)
    counts = cnt[:, 0].astype(I32)
    padded = ((counts + tme - 1) // tme) * tme
    ends = jnp.cumsum(padded)
    off = (ends - padded).astype(I32)
    n_used = (ends[-1:] // tme).astype(I32)
    tile_start = jnp.arange(n_tiles, dtype=I32) * tme
    tile_expert = jnp.minimum(jnp.sum(tile_start[:, None] >= ends[None, :], axis=1), N_EXPERTS - 1).astype(I32)
    xs = _dispatch(off, ert, x, jnp.zeros((n_tiles * tme, D_MODEL), F32), tm)
    ys = _ffn(tile_expert, n_used, xs, w_gu, w_down, tme)
    return _combine(off, ert, ys, wb0, wb1, x, g, b, tm)


def _trunk(x, seq_lens, tm, tme, tt, ln_g, ln_b, a_w_in, a_vn_g, a_vn_b, a_w_s, a_b_s, a_w_out,
           b_w_in, b_conv_w, b_conv_b, b_w_gates, b_b_gates, b_lambda, b_w_out,
           router_w, router_b, moe_w_gu, moe_w_down):
    row = lambda v: v.reshape(1, -1).astype(F32)
    rw = jnp.pad(router_w.astype(F32), ((0, 0), (0, LANES - N_EXPERTS)))
    rwh = rw.astype(BF16)
    rwl = (rw - rwh.astype(F32)).astype(BF16)
    rb = jnp.pad(router_b.astype(F32), (0, LANES - N_EXPERTS)).reshape(1, LANES)
    tri = (jnp.arange(tm)[:, None] < jnp.arange(tm)[None, :]).astype(BF16)
    ia = ib = 0
    for layer in range(DEPTH):
        g0, b0 = row(ln_g[layer, 0]), row(ln_b[layer, 0])
        if layer % 2 == 0:
            bs_full = jnp.repeat(a_b_s[ia].T.astype(F32), CHUNK, axis=1)
            y = _a1(x, a_w_in[ia].astype(BF16), row(a_vn_g[ia]), row(a_vn_b[ia]), a_w_s[ia].astype(BF16), bs_full, tm)
            x = _out_proj(_op_a_body, [y], a_w_out[ia].astype(BF16), x, g0, b0, tm, "op_a")
            ia += 1
        else:
            gg, xr = _b1(x, b_w_in[ib].astype(BF16), tm)
            wgt = b_w_gates[ib].astype(BF16)
            wg = jnp.stack([jnp.concatenate([wgt[0], wgt[1]], axis=-1),
                            jnp.concatenate([wgt[2], wgt[3]], axis=-1)])
            hf, hb = _b2(xr, b_conv_w[ib].astype(F32), row(b_conv_b[ib]), wg, b_b_gates[ib].astype(F32),
                         b_lambda[ib].astype(F32), seq_lens, tt)
            x = _out_proj(_op_b_body, [hf, hb, gg], b_w_out[ib].astype(BF16), x, g0, b0, tm, "op_b")
            ib += 1
        x = _moe_layer(x, rwh, rwl, rb, tri, moe_w_gu[layer].astype(BF16), moe_w_down[layer].astype(BF16),
                       row(ln_g[layer, 1]), row(ln_b[layer, 1]), tm, tme)
    return x


def kernel(x_prompt, x_sample, ln_g, ln_b, a_w_in, a_vn_g, a_vn_b, a_w_s, a_b_s, a_w_out, b_w_in, b_conv_w, b_conv_b,
           b_w_gates, b_b_gates, b_lambda, b_w_out, router_w, router_b, moe_w_gu, moe_w_down):
    d = x_prompt.shape[-1]
    seq_lens = (x_prompt.shape[1],) * x_prompt.shape[0] + (x_sample.shape[1],) * x_sample.shape[0]
    x = jnp.concatenate([x_prompt.reshape(-1, d), x_sample.reshape(-1, d)], axis=0)
    y = _trunk(x, seq_lens, TM, TME, TT, ln_g, ln_b, a_w_in, a_vn_g, a_vn_b, a_w_s, a_b_s, a_w_out,
               b_w_in, b_conv_w, b_conv_b, b_w_gates, b_b_gates, b_lambda, b_w_out,
               router_w, router_b, moe_w_gu, moe_w_down)
    n_p = x_prompt.shape[0] * x_prompt.shape[1]
    return y[:n_p].reshape(x_prompt.shape), y[n_p:].reshape(x_sample.shape)
```

```python
import functools

import jax
import jax.numpy as jnp
from jax import lax
from jax.experimental import pallas as pl
from jax.experimental.pallas import tpu as pltpu

F32 = jnp.float32
BF16 = jnp.bfloat16
I32 = jnp.int32

D_MODEL = 2048
DEPTH = 4
CHUNK = 128
A_HALF = D_MODEL
A_GROUPS = 16
RNN_WIDTH = D_MODEL
RNN_BLOCKS = 16
RNN_BDIM = RNN_WIDTH // RNN_BLOCKS
LRU_C = 8.0
N_EXPERTS = 16
N_GROUPS = 4
EXPERTS_PER_GROUP = N_EXPERTS // N_GROUPS
D_EXPERT = D_MODEL // 2
DEEPNORM_ALPHA = (2.0 * DEPTH) ** 0.25
LN_EPS = 1e-5

LANES = 128
SUBLANES = 8
VMEM_LIMIT = 56 * 1024 * 1024

TM = 512
TME = 512
TT = 256


def _cparams(sem):
    return pltpu.CompilerParams(dimension_semantics=sem, vmem_limit_bytes=VMEM_LIMIT)


def _resident(shape):
    nd = len(shape)
    return pl.BlockSpec(shape, lambda *_: (0,) * nd, pipeline_mode=pl.Buffered(1))


def _dot(a, b):
    return jnp.dot(a, b, preferred_element_type=F32)


def _ln(x, g, b):
    mu = jnp.mean(x, axis=-1, keepdims=True)
    xc = x - mu
    var = jnp.mean(xc * xc, axis=-1, keepdims=True)
    return xc * lax.rsqrt(var + LN_EPS) * g + b


def _sigmoid(x):
    return 0.5 * jnp.tanh(0.5 * x) + 0.5


def _a1_body(x_ref, w_ref, vg_ref, vb_ref, ws_ref, bs_ref, y_ref, s_scr, *, tm):
    xb = x_ref[...].astype(BF16)
    v = jax.nn.gelu(_dot(xb, w_ref[:, A_HALF:]))
    v = _ln(v, vg_ref[...], vb_ref[...]).astype(BF16)
    for c in range(tm // CHUNK):
        rows = slice(c * CHUNK, (c + 1) * CHUNK)
        for g in range(A_GROUPS):
            cols = slice(g * LANES, (g + 1) * LANES)
            s_scr[rows, cols] = _dot(ws_ref[g], v[rows, cols]) + bs_ref[:, cols]
    u = jax.nn.gelu(_dot(xb, w_ref[:, :A_HALF]))
    y_ref[...] = (u * s_scr[...]).astype(BF16)


def _a1(x, w_in, vn_g, vn_b, w_s, bs_full, tm):
    t = x.shape[0]
    return pl.pallas_call(
        functools.partial(_a1_body, tm=tm),
        grid=(t // tm,),
        in_specs=[
            pl.BlockSpec((tm, D_MODEL), lambda i: (i, 0)),
            _resident((D_MODEL, 2 * A_HALF)),
            _resident((1, A_HALF)),
            _resident((1, A_HALF)),
            _resident((A_GROUPS, CHUNK, CHUNK)),
            _resident((CHUNK, A_HALF)),
        ],
        out_specs=pl.BlockSpec((tm, A_HALF), lambda i: (i, 0)),
        out_shape=jax.ShapeDtypeStruct((t, A_HALF), BF16),
        scratch_shapes=[pltpu.VMEM((tm, A_HALF), F32)],
        compiler_params=_cparams(("arbitrary",)),
        name="a1_gmlp_front",
    )(x, w_in, vn_g, vn_b, w_s, bs_full)


def _op_a_body(y_ref, w_ref, x_ref, g_ref, b_ref, o_ref):
    m = _dot(y_ref[...], w_ref[...])
    o_ref[...] = _ln(DEEPNORM_ALPHA * x_ref[...] + m, g_ref[...], b_ref[...])


def _op_b_body(hf_ref, hb_ref, gg_ref, w_ref, x_ref, g_ref, b_ref, o_ref):
    y = ((hf_ref[...] + hb_ref[...]) * gg_ref[...].astype(F32)).astype(BF16)
    m = _dot(y, w_ref[...])
    o_ref[...] = _ln(DEEPNORM_ALPHA * x_ref[...] + m, g_ref[...], b_ref[...])


def _out_proj(body, acts, w_out, x, g, b, tm, name):
    t = x.shape[0]
    tile = pl.BlockSpec((tm, D_MODEL), lambda i: (i, 0))
    return pl.pallas_call(
        body,
        grid=(t // tm,),
        in_specs=[tile] * len(acts) + [_resident(w_out.shape), tile, _resident((1, D_MODEL)), _resident((1, D_MODEL))],
        out_specs=tile,
        out_shape=jax.ShapeDtypeStruct((t, D_MODEL), F32),
        compiler_params=_cparams(("arbitrary",)),
        name=name,
    )(*acts, w_out, x, g, b)


def _b1_body(x_ref, w_ref, gg_ref, xr_ref):
    xb = x_ref[...].astype(BF16)
    gg_ref[...] = jax.nn.gelu(_dot(xb, w_ref[:, :RNN_WIDTH])).astype(BF16)
    xr_ref[...] = _dot(xb, w_ref[:, RNN_WIDTH:])


def _b1(x, w_in, tm):
    t = x.shape[0]
    tile = pl.BlockSpec((tm, D_MODEL), lambda i: (i, 0))
    return pl.pallas_call(
        _b1_body,
        grid=(t // tm,),
        in_specs=[tile, _resident((D_MODEL, 2 * RNN_WIDTH))],
        out_specs=[tile, tile],
        out_shape=[jax.ShapeDtypeStruct((t, RNN_WIDTH), BF16), jax.ShapeDtypeStruct((t, RNN_WIDTH), F32)],
        compiler_params=_cparams(("arbitrary",)),
        name="b1_rglru_front",
    )(x, w_in)


def _scan_rows(a_s, b_s, o_ref, carry_scr, *, tt, reverse):
    n_grp = tt // SUBLANES
    width = 512
    row_id = lax.broadcasted_iota(I32, (SUBLANES, width), 0)

    def step(k, carry):
        g = (n_grp - 1 - k) if reverse else k
        r0 = pl.multiple_of(g * SUBLANES, SUBLANES)
        new = []
        for q in range(RNN_WIDTH // width):
            cols = slice(q * width, (q + 1) * width)
            a = a_s[pl.ds(r0, SUBLANES), cols]
            b = b_s[pl.ds(r0, SUBLANES), cols]
            for d in (1, 2, 4):
                if reverse:
                    keep = row_id < SUBLANES - d
                    sh = SUBLANES - d
                else:
                    keep = row_id >= d
                    sh = d
                a_sh = jnp.where(keep, pltpu.roll(a, sh, axis=0), 1.0)
                b_sh = jnp.where(keep, pltpu.roll(b, sh, axis=0), 0.0)
                b = a * b_sh + b
                a = a * a_sh
            h = a * carry[q] + b
            o_ref[pl.ds(r0, SUBLANES), cols] = h
            edge = h[0:1, :] if reverse else h[SUBLANES - 1:SUBLANES, :]
            new.append(jnp.broadcast_to(edge, (SUBLANES, width)))
        return tuple(new)

    init = tuple(carry_scr[:, q * width:(q + 1) * width] for q in range(RNN_WIDTH // width))
    fin = lax.fori_loop(0, n_grp, step, init)
    for q in range(RNN_WIDTH // width):
        carry_scr[:, q * width:(q + 1) * width] = fin[q]


def _b2_direction(d, x_ref, prev_ref, next_ref, at_start, at_end, reset, cw_ref, cb_ref, wg_ref, bg_ref, sp_ref,
                  o_ref, carry_scr, win, xc_s, a_s, b_s, *, tt):
    win[0:SUBLANES, :] = jnp.where(at_start, 0.0, prev_ref[...])
    win[SUBLANES:SUBLANES + tt, :] = x_ref[...]
    win[SUBLANES + tt:2 * SUBLANES + tt, :] = jnp.where(at_end, 0.0, next_ref[...])
    xc = cb_ref[...]
    for k in range(4):
        xc = xc + cw_ref[k:k + 1, :] * win[SUBLANES - 1 + k:SUBLANES - 1 + k + tt, :]
    xc_s[...] = xc
    for h in range(RNN_BLOCKS):
        cols = slice(h * RNN_BDIM, (h + 1) * RNN_BDIM)
        xh = xc_s[:, cols]
        g2 = _dot(xh.astype(BF16), wg_ref[d, h])
        r = _sigmoid(g2[:, :RNN_BDIM] + bg_ref[2 * d:2 * d + 1, cols])
        ig = _sigmoid(g2[:, RNN_BDIM:] + bg_ref[2 * d + 1:2 * d + 2, cols])
        log_a = (-LRU_C * sp_ref[d:d + 1, cols]) * r
        a = jnp.exp(log_a)
        mult = jnp.sqrt(-jnp.tanh(log_a) * (a * a + 1.0))
        a_s[:, cols] = a
        b_s[:, cols] = mult * (ig * xh)

    @pl.when(reset)
    def _():
        carry_scr[...] = jnp.zeros_like(carry_scr)

    _scan_rows(a_s, b_s, o_ref, carry_scr, tt=tt, reverse=(d == 1))


def _b2_body(xf_ref, pf_ref, nf_ref, xb_ref, pb_ref, nb_ref, cw_ref, cb_ref, wg_ref, bg_ref, lam_ref,
             hf_ref, hb_ref, cf_scr, cbk_scr, sp_scr, win, xc_s, a_s, b_s, *, tt, n_t, starts, ends):
    i = pl.program_id(0)
    j = n_t - 1 - i

    def hits(idx, marks):
        r = idx == marks[0]
        for m in marks[1:]:
            r = jnp.logical_or(r, idx == m)
        return r

    sp_scr[...] = jax.nn.softplus(-lam_ref[...])
    common = dict(cw_ref=cw_ref, cb_ref=cb_ref, wg_ref=wg_ref, bg_ref=bg_ref, sp_ref=sp_scr,
                  win=win, xc_s=xc_s, a_s=a_s, b_s=b_s, tt=tt)
    f_start, f_end = hits(i, starts), hits(i, ends)
    _b2_direction(0, xf_ref, pf_ref, nf_ref, f_start, f_end, f_start, o_ref=hf_ref, carry_scr=cf_scr, **common)
    b_start, b_end = hits(j, starts), hits(j, ends)
    _b2_direction(1, xb_ref, pb_ref, nb_ref, b_start, b_end, b_end, o_ref=hb_ref, carry_scr=cbk_scr, **common)


def _b2(xr, conv_w, conv_b, wg, b_gates, lam, seq_lens, tt):
    t = xr.shape[0]
    n_t = t // tt
    hp = tt // SUBLANES
    n_h = t // SUBLANES
    bounds = [0]
    for s in seq_lens:
        bounds.append(bounds[-1] + s)
    starts = tuple(b // tt for b in bounds[:-1])
    ends = tuple(b // tt - 1 for b in bounds[1:])
    w = RNN_WIDTH
    main_f = pl.BlockSpec((tt, w), lambda i: (i, 0))
    prev_f = pl.BlockSpec((SUBLANES, w), lambda i: (jnp.maximum(i * hp - 1, 0), 0))
    next_f = pl.BlockSpec((SUBLANES, w), lambda i: (jnp.minimum((i + 1) * hp, n_h - 1), 0))
    main_b = pl.BlockSpec((tt, w), lambda i: (n_t - 1 - i, 0))
    prev_b = pl.BlockSpec((SUBLANES, w), lambda i: (jnp.maximum((n_t - 1 - i) * hp - 1, 0), 0))
    next_b = pl.BlockSpec((SUBLANES, w), lambda i: (jnp.minimum((n_t - i) * hp, n_h - 1), 0))
    return pl.pallas_call(
        functools.partial(_b2_body, tt=tt, n_t=n_t, starts=starts, ends=ends),
        grid=(n_t,),
        in_specs=[main_f, prev_f, next_f, main_b, prev_b, next_b,
                  _resident((4, w)), _resident((1, w)), _resident(wg.shape), _resident((4, w)), _resident((2, w))],
        out_specs=[main_f, main_b],
        out_shape=[jax.ShapeDtypeStruct((t, w), F32), jax.ShapeDtypeStruct((t, w), F32)],
        scratch_shapes=[pltpu.VMEM((SUBLANES, w), F32), pltpu.VMEM((SUBLANES, w), F32), pltpu.VMEM((2, w), F32),
                        pltpu.VMEM((tt + 2 * SUBLANES, w), F32), pltpu.VMEM((tt, w), F32),
                        pltpu.VMEM((tt, w), F32), pltpu.VMEM((tt, w), F32)],
        compiler_params=_cparams(("arbitrary",)),
        name="b2_conv_gates_scan",
    )(xr, xr, xr, xr, xr, xr, conv_w, conv_b, wg, b_gates, lam)


def _top2_of4(a):
    m1 = jnp.maximum(jnp.maximum(a[0], a[1]), jnp.maximum(a[2], a[3]))
    i1 = jnp.where(a[0] == m1, 0, jnp.where(a[1] == m1, 1, jnp.where(a[2] == m1, 2, 3)))
    b = [jnp.where(i1 == k, -1.0, a[k]) for k in range(4)]
    m2 = jnp.maximum(jnp.maximum(b[0], b[1]), jnp.maximum(b[2], b[3]))
    i2 = jnp.where(b[0] == m2, 0, jnp.where(b[1] == m2, 1, jnp.where(b[2] == m2, 2, 3)))
    return m1, i1, m2, i2


def _router_body(x_ref, wh_ref, wl_ref, rb_ref, tri_ref, ert_ref, wb0_ref, wb1_ref, cnt_ref, base_scr, *, tm):
    @pl.when(pl.program_id(0) == 0)
    def _():
        base_scr[...] = jnp.zeros_like(base_scr)

    x = x_ref[...]
    xh = x.astype(BF16)
    xl = (x - xh.astype(F32)).astype(BF16)
    wh = wh_ref[...]
    logits = _dot(xh, wh) + _dot(xl, wh) + _dot(xh, wl_ref[...]) + rb_ref[...]
    lt = jnp.transpose(logits)
    l = [lt[e:e + 1, :] for e in range(N_EXPERTS)]
    mx = l[0]
    for e in range(1, N_EXPERTS):
        mx = jnp.maximum(mx, l[e])
    ex = [jnp.exp(v - mx) for v in l]
    den = ex[0]
    for e in range(1, N_EXPERTS):
        den = den + ex[e]
    p = [v / den for v in ex]

    tops = [_top2_of4(p[g * EXPERTS_PER_GROUP:(g + 1) * EXPERTS_PER_GROUP]) for g in range(N_GROUPS)]
    score = [t[0] + t[2] for t in tops]
    best = jnp.maximum(jnp.maximum(score[0], score[1]), jnp.maximum(score[2], score[3]))
    gsel = jnp.where(score[0] == best, 0, jnp.where(score[1] == best, 1, jnp.where(score[2] == best, 2, 3)))

    def pick(field):
        return jnp.where(gsel == 0, tops[0][field],
                         jnp.where(gsel == 1, tops[1][field], jnp.where(gsel == 2, tops[2][field], tops[3][field])))

    v1, i1, v2, i2 = pick(0), pick(1), pick(2), pick(3)
    e1 = gsel * EXPERTS_PER_GROUP + i1
    e2 = gsel * EXPERTS_PER_GROUP + i2
    vs = v1 + v2
    w1 = v1 / vs
    w2 = v2 / vs

    eid = lax.broadcasted_iota(I32, (N_EXPERTS, tm), 0)
    hit1 = eid == e1
    hit2 = eid == e2
    oh = jnp.where(hit1, 1.0, 0.0) + jnp.where(hit2, 1.0, 0.0)
    before = _dot(oh.astype(BF16), tri_ref[...])
    tot = before + jnp.concatenate([base_scr[...]] * (tm // LANES), axis=1)
    r1 = jnp.sum(jnp.where(hit1, tot, 0.0), axis=0, keepdims=True)
    r2 = jnp.sum(jnp.where(hit2, tot, 0.0), axis=0, keepdims=True)
    base_scr[...] = base_scr[...] + jnp.sum(oh, axis=1, keepdims=True)
    cnt_ref[...] = base_scr[...]

    zero = jnp.zeros((1, tm), I32)
    ert_ref[...] = jnp.concatenate([e1, e2, r1.astype(I32), r2.astype(I32), zero, zero, zero, zero], axis=0)
    wb0_ref[...] = jnp.transpose(jnp.broadcast_to(w1, (LANES, tm)))
    wb1_ref[...] = jnp.transpose(jnp.broadcast_to(w2, (LANES, tm)))


def _router(x, rwh, rwl, rb, tri, tm):
    t = x.shape[0]
    return pl.pallas_call(
        functools.partial(_router_body, tm=tm),
        grid=(t // tm,),
        in_specs=[pl.BlockSpec((tm, D_MODEL), lambda i: (i, 0)),
                  _resident((D_MODEL, LANES)), _resident((D_MODEL, LANES)), _resident((1, LANES)), _resident((tm, tm))],
        out_specs=[pl.BlockSpec((SUBLANES, tm), lambda i: (0, i)),
                   pl.BlockSpec((tm, LANES), lambda i: (i, 0)),
                   pl.BlockSpec((tm, LANES), lambda i: (i, 0)),
                   pl.BlockSpec((N_EXPERTS, LANES), lambda i: (0, 0))],
        out_shape=[jax.ShapeDtypeStruct((SUBLANES, t), I32),
                   jax.ShapeDtypeStruct((t, LANES), F32),
                   jax.ShapeDtypeStruct((t, LANES), F32),
                   jax.ShapeDtypeStruct((N_EXPERTS, LANES), F32)],
        scratch_shapes=[pltpu.VMEM((N_EXPERTS, LANES), F32)],
        compiler_params=_cparams(("arbitrary",)),
        name="moe_router",
    )(x, rwh, rwl, rb, tri)


def _row_copy(src, s, dst, d, sem):
    return pltpu.make_async_copy(src.at[pl.ds(s, 1), :], dst.at[pl.ds(d, 1), :], sem)


def _rows_wait(src, dst, n, sem):
    pltpu.make_async_copy(src.at[pl.ds(0, n), :], dst.at[pl.ds(0, n), :], sem).wait()


def _dispatch_body(pad_lo_ref, pad_hi_ref, pos_ref, x_ref, xs_out, zrow, sem, zsem, *, tm):
    @pl.when(pl.program_id(0) == 0)
    def _():
        zrow[...] = jnp.zeros_like(zrow)
        for e in range(N_EXPERTS):
            def put(r, c):
                _row_copy(zrow, 0, xs_out, r, zsem).start()
                return c

            lax.fori_loop(pad_lo_ref[e], pad_hi_ref[e], put, 0)
        for e in range(N_EXPERTS):
            def got(r, c):
                _row_copy(zrow, 0, xs_out, 0, zsem).wait()
                return c

            lax.fori_loop(pad_lo_ref[e], pad_hi_ref[e], got, 0)

    def issue(t, c):
        for k in range(2):
            _row_copy(x_ref, t, xs_out, pos_ref[k, t], sem).start()
        return c

    lax.fori_loop(0, tm, issue, 0)
    for k in range(2):
        _rows_wait(x_ref, xs_out, tm, sem)


def _dispatch(pad_lo, pad_hi, pos, x, n_rows, tm):
    t = x.shape[0]
    return pl.pallas_call(
        functools.partial(_dispatch_body, tm=tm),
        grid_spec=pltpu.PrefetchScalarGridSpec(
            num_scalar_prefetch=2,
            grid=(t // tm,),
            in_specs=[pl.BlockSpec((2, tm), lambda i, lo, hi: (0, i), memory_space=pltpu.SMEM),
                      pl.BlockSpec((tm, D_MODEL), lambda i, lo, hi: (i, 0))],
            out_specs=pl.BlockSpec(memory_space=pl.ANY),
            scratch_shapes=[pltpu.VMEM((SUBLANES, D_MODEL), F32),
                            pltpu.SemaphoreType.DMA(()), pltpu.SemaphoreType.DMA(())]),
        out_shape=jax.ShapeDtypeStruct((n_rows, D_MODEL), F32),
        compiler_params=pltpu.CompilerParams(dimension_semantics=("arbitrary",), vmem_limit_bytes=VMEM_LIMIT,
                                             has_side_effects=True, disable_bounds_checks=True),
        name="moe_dispatch",
    )(pad_lo, pad_hi, pos, x)


def _ffn_body(te_ref, nu_ref, xs_ref, wgu_ref, wdn_ref, ys_ref):
    used = pl.program_id(0) < nu_ref[0]

    @pl.when(used)
    def _():
        h = _dot(xs_ref[...].astype(BF16), wgu_ref[0])
        hh = (jax.nn.silu(h[:, :D_EXPERT]) * h[:, D_EXPERT:]).astype(BF16)
        ys_ref[...] = _dot(hh, wdn_ref[0])

    @pl.when(jnp.logical_not(used))
    def _():
        ys_ref[...] = jnp.zeros_like(ys_ref)


def _ffn(tile_expert, n_used, xs, w_gu, w_down, tm):
    p = xs.shape[0]
    row_tile = pl.BlockSpec((tm, D_MODEL), lambda i, te, nu: (i, 0))
    return pl.pallas_call(
        _ffn_body,
        grid_spec=pltpu.PrefetchScalarGridSpec(
            num_scalar_prefetch=2,
            grid=(p // tm,),
            in_specs=[row_tile,
                      pl.BlockSpec((1, D_MODEL, 2 * D_EXPERT), lambda i, te, nu: (te[i], 0, 0)),
                      pl.BlockSpec((1, D_EXPERT, D_MODEL), lambda i, te, nu: (te[i], 0, 0))],
            out_specs=row_tile),
        out_shape=jax.ShapeDtypeStruct((p, D_MODEL), F32),
        compiler_params=_cparams(("arbitrary",)),
        name="moe_expert_ffn",
    )(tile_expert, n_used, xs, w_gu, w_down)


def _combine_body(pos_ref, posn_ref, ys_hbm, wb0_ref, wb1_ref, x_ref, g_ref, b_ref, o_ref,
                  a0, a1, b0, b1, sem_a, sem_b, *, tm, n_t):
    i = pl.program_id(0)

    def gather(p_ref, bufs, sem):
        def issue(t, c):
            for k in range(2):
                _row_copy(ys_hbm, p_ref[k, t], bufs[k], t, sem).start()
            return c

        lax.fori_loop(0, tm, issue, 0)

    def reduce(bufs, sem):
        for k in range(2):
            _rows_wait(ys_hbm, bufs[k], tm, sem)
        reps = D_MODEL // LANES
        w0 = jnp.concatenate([wb0_ref[...]] * reps, axis=1)
        w1 = jnp.concatenate([wb1_ref[...]] * reps, axis=1)
        f = w0 * bufs[0][...] + w1 * bufs[1][...]
        o_ref[...] = _ln(DEEPNORM_ALPHA * x_ref[...] + f, g_ref[...], b_ref[...])

    slots = (((a0, a1), sem_a), ((b0, b1), sem_b))

    @pl.when(i == 0)
    def _():
        gather(pos_ref, *slots[0])

    for s in range(2):
        @pl.when(i % 2 == s)
        def _():
            @pl.when(i + 1 < n_t)
            def _():
                gather(posn_ref, *slots[1 - s])

            reduce(*slots[s])


def _combine(pos, ys, wb0, wb1, x, g, b, tm):
    t = x.shape[0]
    n_t = t // tm
    tile = pl.BlockSpec((tm, D_MODEL), lambda i: (i, 0))
    wtile = pl.BlockSpec((tm, LANES), lambda i: (i, 0))
    vec = pl.BlockSpec((1, D_MODEL), lambda i: (0, 0))
    rows = pltpu.VMEM((tm, D_MODEL), F32)
    return pl.pallas_call(
        functools.partial(_combine_body, tm=tm, n_t=n_t),
        grid=(n_t,),
        in_specs=[pl.BlockSpec((2, tm), lambda i: (0, i), memory_space=pltpu.SMEM),
                  pl.BlockSpec((2, tm), lambda i: (0, jnp.minimum(i + 1, n_t - 1)), memory_space=pltpu.SMEM),
                  pl.BlockSpec(memory_space=pl.ANY), wtile, wtile, tile, vec, vec],
        out_specs=tile,
        scratch_shapes=[rows, rows, rows, rows, pltpu.SemaphoreType.DMA(()), pltpu.SemaphoreType.DMA(())],
        out_shape=jax.ShapeDtypeStruct((t, D_MODEL), F32),
        compiler_params=pltpu.CompilerParams(dimension_semantics=("arbitrary",), vmem_limit_bytes=VMEM_LIMIT,
                                             disable_bounds_checks=True),
        name="moe_combine",
    )(pos, pos, ys, wb0, wb1, x, g, b)


def _moe_layer(x, rwh, rwl, rb, tri, w_gu, w_down, g, b, tm, tme):
    t = x.shape[0]
    n_tiles = (2 * t) // tme + N_EXPERTS
    ert, wb0, wb1, cnt = _router(x, rwh, rwl, rb, tri, tm)
    counts = cnt[:, 0].astype(I32)
    padded = ((counts + tme - 1) // tme) * tme
    ends = jnp.cumsum(padded).astype(I32)
    off = ends - padded
    n_used = ends[-1:] // tme
    tile_start = jnp.arange(n_tiles, dtype=I32) * tme
    tile_expert = jnp.minimum(jnp.sum(tile_start[:, None] >= ends[None, :], axis=1), N_EXPERTS - 1).astype(I32)
    pos = off[ert[0:2]] + ert[2:4]
    pad_hi = ends.at[N_EXPERTS - 1].set(n_tiles * tme)
    xs = _dispatch(off + counts, pad_hi, pos, x, n_tiles * tme, tm)
    ys = _ffn(tile_expert, n_used, xs, w_gu, w_down, tme)
    return _combine(pos, ys, wb0, wb1, x, g, b, tm)


def _trunk(x, seq_lens, tm, tme, tt, ln_g, ln_b, a_w_in, a_vn_g, a_vn_b, a_w_s, a_b_s, a_w_out,
           b_w_in, b_conv_w, b_conv_b, b_w_gates, b_b_gates, b_lambda, b_w_out,
           router_w, router_b, moe_w_gu, moe_w_down):
    row = lambda v: v.reshape(1, -1).astype(F32)
    rw = jnp.pad(router_w.astype(F32), ((0, 0), (0, LANES - N_EXPERTS)))
    rwh = rw.astype(BF16)
    rwl = (rw - rwh.astype(F32)).astype(BF16)
    rb = jnp.pad(router_b.astype(F32), (0, LANES - N_EXPERTS)).reshape(1, LANES)
    tri = (jnp.arange(tm)[:, None] < jnp.arange(tm)[None, :]).astype(BF16)
    ia = ib = 0
    for layer in range(DEPTH):
        g0, b0 = row(ln_g[layer, 0]), row(ln_b[layer, 0])
        if layer % 2 == 0:
            bs_full = jnp.repeat(a_b_s[ia].T.astype(F32), CHUNK, axis=1)
            y = _a1(x, a_w_in[ia].astype(BF16), row(a_vn_g[ia]), row(a_vn_b[ia]), a_w_s[ia].astype(BF16), bs_full, tm)
            x = _out_proj(_op_a_body, [y], a_w_out[ia].astype(BF16), x, g0, b0, tm, "op_a")
            ia += 1
        else:
            gg, xr = _b1(x, b_w_in[ib].astype(BF16), tm)
            wgt = b_w_gates[ib].astype(BF16)
            wg = jnp.stack([jnp.concatenate([wgt[0], wgt[1]], axis=-1),
                            jnp.concatenate([wgt[2], wgt[3]], axis=-1)])
            hf, hb = _b2(xr, b_conv_w[ib].astype(F32), row(b_conv_b[ib]), wg, b_b_gates[ib].astype(F32),
                         b_lambda[ib].astype(F32), seq_lens, tt)
            x = _out_proj(_op_b_body, [hf, hb, gg], b_w_out[ib].astype(BF16), x, g0, b0, tm, "op_b")
            ib += 1
        x = _moe_layer(x, rwh, rwl, rb, tri, moe_w_gu[layer].astype(BF16), moe_w_down[layer].astype(BF16),
                       row(ln_g[layer, 1]), row(ln_b[layer, 1]), tm, tme)
    return x


def kernel(x_prompt, x_sample, ln_g, ln_b, a_w_in, a_vn_g, a_vn_b, a_w_s, a_b_s, a_w_out, b_w_in, b_conv_w, b_conv_b,
           b_w_gates, b_b_gates, b_lambda, b_w_out, router_w, router_b, moe_w_gu, moe_w_down):
    d = x_prompt.shape[-1]
    seq_lens = (x_prompt.shape[1],) * x_prompt.shape[0] + (x_sample.shape[1],) * x_sample.shape[0]
    x = jnp.concatenate([x_prompt.reshape(-1, d), x_sample.reshape(-1, d)], axis=0)
    y = _trunk(x, seq_lens, TM, TME, TT, ln_g, ln_b, a_w_in, a_vn_g, a_vn_b, a_w_s, a_b_s, a_w_out,
               b_w_in, b_conv_w, b_conv_b, b_w_gates, b_b_gates, b_lambda, b_w_out,
               router_w, router_b, moe_w_gu, moe_w_down)
    n_p = x_prompt.shape[0] * x_prompt.shape[1]
    return y[:n_p].reshape(x_prompt.shape), y[n_p:].reshape(x_sample.shape)
```

```python
import functools

import jax
import jax.numpy as jnp
from jax import lax
from jax.experimental import pallas as pl
from jax.experimental.pallas import tpu as pltpu

F32 = jnp.float32
BF16 = jnp.bfloat16
I32 = jnp.int32

D_MODEL = 2048
DEPTH = 4
CHUNK = 128
A_HALF = D_MODEL
A_GROUPS = 16
RNN_WIDTH = D_MODEL
RNN_BLOCKS = 16
RNN_BDIM = RNN_WIDTH // RNN_BLOCKS
LRU_C = 8.0
N_EXPERTS = 16
N_GROUPS = 4
EXPERTS_PER_GROUP = N_EXPERTS // N_GROUPS
D_EXPERT = D_MODEL // 2
DEEPNORM_ALPHA = (2.0 * DEPTH) ** 0.25
LN_EPS = 1e-5

LANES = 128
SUBLANES = 8
VMEM_LIMIT = 56 * 1024 * 1024

TM = 512
TME = 512
TT = 256


def _cparams(sem):
    return pltpu.CompilerParams(dimension_semantics=sem, vmem_limit_bytes=VMEM_LIMIT)


def _resident(shape):
    nd = len(shape)
    return pl.BlockSpec(shape, lambda *_: (0,) * nd, pipeline_mode=pl.Buffered(1))


def _dot(a, b):
    return jnp.dot(a, b, preferred_element_type=F32)


def _ln(x, g, b):
    mu = jnp.mean(x, axis=-1, keepdims=True)
    xc = x - mu
    var = jnp.mean(xc * xc, axis=-1, keepdims=True)
    return xc * lax.rsqrt(var + LN_EPS) * g + b


def _tok_specs(pieces, tm):
    if len(pieces) == 1:
        return [pl.BlockSpec((tm, D_MODEL), lambda i: (i, 0))]
    n_first = pieces[0].shape[0] // tm
    return [pl.BlockSpec((tm, D_MODEL), lambda i: (jnp.minimum(i, n_first - 1), 0)),
            pl.BlockSpec((tm, D_MODEL), lambda i: (jnp.maximum(i - n_first, 0), 0))]


def _tok_tile(refs, n_first):
    if len(refs) == 1:
        return refs[0][...]
    return jnp.where(pl.program_id(0) < n_first, refs[0][...], refs[1][...])


def _a1_body(*refs, tm, n_x, n_first):
    w_ref, vg_ref, vb_ref, ws_ref, bs_ref, y_ref, s_scr = refs[n_x:]
    xb = _tok_tile(refs[:n_x], n_first).astype(BF16)
    v = jax.nn.gelu(_dot(xb, w_ref[:, A_HALF:]))
    v = _ln(v, vg_ref[...], vb_ref[...]).astype(BF16)
    for c in range(tm // CHUNK):
        rows = slice(c * CHUNK, (c + 1) * CHUNK)
        for g in range(A_GROUPS):
            cols = slice(g * LANES, (g + 1) * LANES)
            s_scr[rows, cols] = _dot(ws_ref[g], v[rows, cols]) + bs_ref[:, cols]
    u = jax.nn.gelu(_dot(xb, w_ref[:, :A_HALF]))
    y_ref[...] = (u * s_scr[...]).astype(BF16)


def _a1(xs, w_in, vn_g, vn_b, w_s, bs_full, tm):
    t = sum(x.shape[0] for x in xs)
    return pl.pallas_call(
        functools.partial(_a1_body, tm=tm, n_x=len(xs), n_first=xs[0].shape[0] // tm),
        grid=(t // tm,),
        in_specs=_tok_specs(xs, tm) + [
            _resident((D_MODEL, 2 * A_HALF)),
            _resident((1, A_HALF)),
            _resident((1, A_HALF)),
            _resident((A_GROUPS, CHUNK, CHUNK)),
            _resident((CHUNK, A_HALF)),
        ],
        out_specs=pl.BlockSpec((tm, A_HALF), lambda i: (i, 0)),
        out_shape=jax.ShapeDtypeStruct((t, A_HALF), BF16),
        scratch_shapes=[pltpu.VMEM((tm, A_HALF), F32)],
        compiler_params=_cparams(("arbitrary",)),
        name="a1_gmlp_front",
    )(*xs, w_in, vn_g, vn_b, w_s, bs_full)


def _op_a_body(*refs, n_x, n_first):
    y_ref, w_ref = refs[:2]
    g_ref, b_ref, o_ref = refs[2 + n_x:]
    m = _dot(y_ref[...], w_ref[...])
    o_ref[...] = _ln(DEEPNORM_ALPHA * _tok_tile(refs[2:2 + n_x], n_first) + m, g_ref[...], b_ref[...])


def _op_b_body(*refs, n_x, n_first):
    hf_ref, hb_ref, gg_ref, w_ref = refs[:4]
    g_ref, b_ref, o_ref = refs[4 + n_x:]
    y = ((hf_ref[...] + hb_ref[...]) * gg_ref[...].astype(F32)).astype(BF16)
    m = _dot(y, w_ref[...])
    o_ref[...] = _ln(DEEPNORM_ALPHA * _tok_tile(refs[4:4 + n_x], n_first) + m, g_ref[...], b_ref[...])


def _out_proj(body, acts, w_out, xs, g, b, tm, name):
    t = sum(x.shape[0] for x in xs)
    tile = pl.BlockSpec((tm, D_MODEL), lambda i: (i, 0))
    return pl.pallas_call(
        functools.partial(body, n_x=len(xs), n_first=xs[0].shape[0] // tm),
        grid=(t // tm,),
        in_specs=([tile] * len(acts) + [_resident(w_out.shape)] + _tok_specs(xs, tm)
                  + [_resident((1, D_MODEL)), _resident((1, D_MODEL))]),
        out_specs=tile,
        out_shape=jax.ShapeDtypeStruct((t, D_MODEL), F32),
        compiler_params=_cparams(("arbitrary",)),
        name=name,
    )(*acts, w_out, *xs, g, b)


def _b1_body(x_ref, w_ref, gg_ref, xr_ref):
    xb = x_ref[...].astype(BF16)
    gg_ref[...] = jax.nn.gelu(_dot(xb, w_ref[:, :RNN_WIDTH])).astype(BF16)
    xr_ref[...] = _dot(xb, w_ref[:, RNN_WIDTH:])


def _b1(x, w_in, tm):
    t = x.shape[0]
    tile = pl.BlockSpec((tm, D_MODEL), lambda i: (i, 0))
    return pl.pallas_call(
        _b1_body,
        grid=(t // tm,),
        in_specs=[tile, _resident((D_MODEL, 2 * RNN_WIDTH))],
        out_specs=[tile, tile],
        out_shape=[jax.ShapeDtypeStruct((t, RNN_WIDTH), BF16), jax.ShapeDtypeStruct((t, RNN_WIDTH), F32)],
        compiler_params=_cparams(("arbitrary",)),
        name="b1_rglru_front",
    )(x, w_in)


def _scan_rows(a_s, b_s, o_ref, carry_scr, *, tt, reverse):
    n_grp = tt // SUBLANES
    width = 512
    row_id = lax.broadcasted_iota(I32, (SUBLANES, width), 0)

    def step(k, carry):
        g = (n_grp - 1 - k) if reverse else k
        r0 = pl.multiple_of(g * SUBLANES, SUBLANES)
        new = []
        for q in range(RNN_WIDTH // width):
            cols = slice(q * width, (q + 1) * width)
            a = a_s[pl.ds(r0, SUBLANES), cols]
            b = b_s[pl.ds(r0, SUBLANES), cols]
            for d in (1, 2, 4):
                if reverse:
                    keep = row_id < SUBLANES - d
                    sh = SUBLANES - d
                else:
                    keep = row_id >= d
                    sh = d
                a_sh = jnp.where(keep, pltpu.roll(a, sh, axis=0), 1.0)
                b_sh = jnp.where(keep, pltpu.roll(b, sh, axis=0), 0.0)
                b = a * b_sh + b
                a = a * a_sh
            h = a * carry[q] + b
            o_ref[pl.ds(r0, SUBLANES), cols] = h
            edge = h[0:1, :] if reverse else h[SUBLANES - 1:SUBLANES, :]
            new.append(jnp.broadcast_to(edge, (SUBLANES, width)))
        return tuple(new)

    init = tuple(carry_scr[:, q * width:(q + 1) * width] for q in range(RNN_WIDTH // width))
    fin = lax.fori_loop(0, n_grp, step, init)
    for q in range(RNN_WIDTH // width):
        carry_scr[:, q * width:(q + 1) * width] = fin[q]


def _b2_direction(d, x_ref, prev_ref, next_ref, at_start, at_end, reset, cw_ref, cb_ref, wg_ref, bg_ref, sp_ref,
                  o_ref, carry_scr, xc_s, a_s, b_s, *, tt):
    x = x_ref[...]
    prev = jnp.where(at_start, 0.0, prev_ref[...])
    nxt = jnp.where(at_end, 0.0, next_ref[...])
    rid = lax.broadcasted_iota(I32, (SUBLANES, 1), 0)
    last = tt - SUBLANES
    xm1 = pltpu.roll(x, 1, axis=0)
    xm1 = jnp.concatenate([jnp.where(rid == 0, prev[SUBLANES - 1:SUBLANES, :], xm1[0:SUBLANES, :]), xm1[SUBLANES:, :]], axis=0)
    xp1 = pltpu.roll(x, tt - 1, axis=0)
    xp1 = jnp.concatenate([xp1[:last, :], jnp.where(rid == SUBLANES - 1, nxt[0:1, :], xp1[last:, :])], axis=0)
    xp2 = pltpu.roll(x, tt - 2, axis=0)
    tail = jnp.where(rid == SUBLANES - 2, nxt[0:1, :], jnp.where(rid == SUBLANES - 1, nxt[1:2, :], xp2[last:, :]))
    xp2 = jnp.concatenate([xp2[:last, :], tail], axis=0)
    xc_s[...] = (cb_ref[...] + cw_ref[0:1, :] * xm1 + cw_ref[1:2, :] * x
                 + cw_ref[2:3, :] * xp1 + cw_ref[3:4, :] * xp2)
    for h in range(RNN_BLOCKS):
        cols = slice(h * RNN_BDIM, (h + 1) * RNN_BDIM)
        xh = xc_s[:, cols]
        g2 = _dot(xh.astype(BF16), wg_ref[d, h])
        t_r = jnp.tanh(g2[:, :RNN_BDIM] + bg_ref[2 * d:2 * d + 1, cols])
        t_i = jnp.tanh(g2[:, RNN_BDIM:] + bg_ref[2 * d + 1:2 * d + 2, cols])
        c2 = sp_ref[d:d + 1, cols]
        log_a = c2 * t_r + c2
        a = jnp.exp(log_a)
        mult = jnp.sqrt(-jnp.tanh(log_a) * (a * a + 1.0))
        a_s[:, cols] = a
        b_s[:, cols] = mult * ((0.5 * t_i + 0.5) * xh)

    @pl.when(reset)
    def _():
        carry_scr[...] = jnp.zeros_like(carry_scr)

    _scan_rows(a_s, b_s, o_ref, carry_scr, tt=tt, reverse=(d == 1))


def _b2_body(xf_ref, pf_ref, nf_ref, xb_ref, pb_ref, nb_ref, cw_ref, cb_ref, wg_ref, bg_ref, lam_ref,
             hf_ref, hb_ref, cf_scr, cbk_scr, sp_scr, xc_s, a_s, b_s, *, tt, n_t, starts, ends):
    i = pl.program_id(0)
    j = n_t - 1 - i

    def hits(idx, marks):
        r = idx == marks[0]
        for m in marks[1:]:
            r = jnp.logical_or(r, idx == m)
        return r

    sp_scr[...] = (-0.5 * LRU_C) * jax.nn.softplus(-lam_ref[...])
    common = dict(cw_ref=cw_ref, cb_ref=cb_ref, wg_ref=wg_ref, bg_ref=bg_ref, sp_ref=sp_scr,
                  xc_s=xc_s, a_s=a_s, b_s=b_s, tt=tt)
    f_start, f_end = hits(i, starts), hits(i, ends)
    _b2_direction(0, xf_ref, pf_ref, nf_ref, f_start, f_end, f_start, o_ref=hf_ref, carry_scr=cf_scr, **common)
    b_start, b_end = hits(j, starts), hits(j, ends)
    _b2_direction(1, xb_ref, pb_ref, nb_ref, b_start, b_end, b_end, o_ref=hb_ref, carry_scr=cbk_scr, **common)


def _b2(xr, conv_w, conv_b, wg, b_gates, lam, seq_lens, tt):
    t = xr.shape[0]
    n_t = t // tt
    hp = tt // SUBLANES
    n_h = t // SUBLANES
    bounds = [0]
    for s in seq_lens:
        bounds.append(bounds[-1] + s)
    starts = tuple(b // tt for b in bounds[:-1])
    ends = tuple(b // tt - 1 for b in bounds[1:])
    w = RNN_WIDTH
    main_f = pl.BlockSpec((tt, w), lambda i: (i, 0))
    prev_f = pl.BlockSpec((SUBLANES, w), lambda i: (jnp.maximum(i * hp - 1, 0), 0))
    next_f = pl.BlockSpec((SUBLANES, w), lambda i: (jnp.minimum((i + 1) * hp, n_h - 1), 0))
    main_b = pl.BlockSpec((tt, w), lambda i: (n_t - 1 - i, 0))
    prev_b = pl.BlockSpec((SUBLANES, w), lambda i: (jnp.maximum((n_t - 1 - i) * hp - 1, 0), 0))
    next_b = pl.BlockSpec((SUBLANES, w), lambda i: (jnp.minimum((n_t - i) * hp, n_h - 1), 0))
    return pl.pallas_call(
        functools.partial(_b2_body, tt=tt, n_t=n_t, starts=starts, ends=ends),
        grid=(n_t,),
        in_specs=[main_f, prev_f, next_f, main_b, prev_b, next_b,
                  _resident((4, w)), _resident((1, w)), _resident(wg.shape), _resident((4, w)), _resident((2, w))],
        out_specs=[main_f, main_b],
        out_shape=[jax.ShapeDtypeStruct((t, w), F32), jax.ShapeDtypeStruct((t, w), F32)],
        scratch_shapes=[pltpu.VMEM((SUBLANES, w), F32), pltpu.VMEM((SUBLANES, w), F32), pltpu.VMEM((2, w), F32),
                        pltpu.VMEM((tt, w), F32), pltpu.VMEM((tt, w), F32), pltpu.VMEM((tt, w), F32)],
        compiler_params=_cparams(("arbitrary",)),
        name="b2_conv_gates_scan",
    )(xr, xr, xr, xr, xr, xr, conv_w, conv_b, wg, b_gates, lam)


def _top2_of4(a):
    m1 = jnp.maximum(jnp.maximum(a[0], a[1]), jnp.maximum(a[2], a[3]))
    i1 = jnp.where(a[0] == m1, 0, jnp.where(a[1] == m1, 1, jnp.where(a[2] == m1, 2, 3)))
    b = [jnp.where(i1 == k, -1.0, a[k]) for k in range(4)]
    m2 = jnp.maximum(jnp.maximum(b[0], b[1]), jnp.maximum(b[2], b[3]))
    i2 = jnp.where(b[0] == m2, 0, jnp.where(b[1] == m2, 1, jnp.where(b[2] == m2, 2, 3)))
    return m1, i1, m2, i2


def _router_body(x_ref, wh_ref, wl_ref, rb_ref, tri_ref, ert_ref, wb0_ref, wb1_ref, cnt_ref, base_scr, *, tm):
    @pl.when(pl.program_id(0) == 0)
    def _():
        base_scr[...] = jnp.zeros_like(base_scr)

    x = x_ref[...]
    xh = x.astype(BF16)
    xl = (x - xh.astype(F32)).astype(BF16)
    wh = wh_ref[...]
    logits = _dot(xh, wh) + _dot(xl, wh) + _dot(xh, wl_ref[...]) + rb_ref[...]
    lt = jnp.transpose(logits)
    l = [lt[e:e + 1, :] for e in range(N_EXPERTS)]
    mx = l[0]
    for e in range(1, N_EXPERTS):
        mx = jnp.maximum(mx, l[e])
    ex = [jnp.exp(v - mx) for v in l]
    den = ex[0]
    for e in range(1, N_EXPERTS):
        den = den + ex[e]
    p = [v / den for v in ex]

    tops = [_top2_of4(p[g * EXPERTS_PER_GROUP:(g + 1) * EXPERTS_PER_GROUP]) for g in range(N_GROUPS)]
    score = [t[0] + t[2] for t in tops]
    best = jnp.maximum(jnp.maximum(score[0], score[1]), jnp.maximum(score[2], score[3]))
    gsel = jnp.where(score[0] == best, 0, jnp.where(score[1] == best, 1, jnp.where(score[2] == best, 2, 3)))

    def pick(field):
        return jnp.where(gsel == 0, tops[0][field],
                         jnp.where(gsel == 1, tops[1][field], jnp.where(gsel == 2, tops[2][field], tops[3][field])))

    v1, i1, v2, i2 = pick(0), pick(1), pick(2), pick(3)
    e1 = gsel * EXPERTS_PER_GROUP + i1
    e2 = gsel * EXPERTS_PER_GROUP + i2
    vs = v1 + v2
    w1 = v1 / vs
    w2 = v2 / vs

    eid = lax.broadcasted_iota(I32, (N_EXPERTS, tm), 0)
    hit1 = eid == e1
    hit2 = eid == e2
    oh = jnp.where(hit1, 1.0, 0.0) + jnp.where(hit2, 1.0, 0.0)
    before = _dot(oh.astype(BF16), tri_ref[...])
    tot = before + jnp.concatenate([base_scr[...]] * (tm // LANES), axis=1)
    r1 = jnp.sum(jnp.where(hit1, tot, 0.0), axis=0, keepdims=True)
    r2 = jnp.sum(jnp.where(hit2, tot, 0.0), axis=0, keepdims=True)
    base_scr[...] = base_scr[...] + jnp.sum(oh, axis=1, keepdims=True)
    cnt_ref[...] = base_scr[...]

    zero = jnp.zeros((1, tm), I32)
    ert_ref[...] = jnp.concatenate([e1, e2, r1.astype(I32), r2.astype(I32), zero, zero, zero, zero], axis=0)
    wb0_ref[...] = jnp.transpose(jnp.broadcast_to(w1, (LANES, tm)))
    wb1_ref[...] = jnp.transpose(jnp.broadcast_to(w2, (LANES, tm)))


def _router(x, rwh, rwl, rb, tri, tm):
    t = x.shape[0]
    return pl.pallas_call(
        functools.partial(_router_body, tm=tm),
        grid=(t // tm,),
        in_specs=[pl.BlockSpec((tm, D_MODEL), lambda i: (i, 0)),
                  _resident((D_MODEL, LANES)), _resident((D_MODEL, LANES)), _resident((1, LANES)), _resident((tm, tm))],
        out_specs=[pl.BlockSpec((SUBLANES, tm), lambda i: (0, i)),
                   pl.BlockSpec((tm, LANES), lambda i: (i, 0)),
                   pl.BlockSpec((tm, LANES), lambda i: (i, 0)),
                   pl.BlockSpec((N_EXPERTS, LANES), lambda i: (0, 0))],
        out_shape=[jax.ShapeDtypeStruct((SUBLANES, t), I32),
                   jax.ShapeDtypeStruct((t, LANES), F32),
                   jax.ShapeDtypeStruct((t, LANES), F32),
                   jax.ShapeDtypeStruct((N_EXPERTS, LANES), F32)],
        scratch_shapes=[pltpu.VMEM((N_EXPERTS, LANES), F32)],
        compiler_params=_cparams(("arbitrary",)),
        name="moe_router",
    )(x, rwh, rwl, rb, tri)


def _row_copy(src, s, dst, d, sem):
    return pltpu.make_async_copy(src.at[pl.ds(s, 1), :], dst.at[pl.ds(d, 1), :], sem)


def _rows_wait(hbm, n, sem):
    pltpu.make_async_copy(hbm.at[pl.ds(0, n), :], hbm.at[pl.ds(0, n), :], sem).wait()


def _dispatch_body(pad_lo_ref, pad_hi_ref, p0_ref, p1_ref, x_ref, xs_out, zrow, sem, zsem, *, tm):
    @pl.when(pl.program_id(0) == 0)
    def _():
        zrow[...] = jnp.zeros_like(zrow)
        for e in range(N_EXPERTS):
            def put(r, c):
                _row_copy(zrow, 0, xs_out, r, zsem).start()
                return c

            lax.fori_loop(pad_lo_ref[e], pad_hi_ref[e], put, 0)
        for e in range(N_EXPERTS):
            def got(r, c):
                _row_copy(zrow, 0, xs_out, 0, zsem).wait()
                return c

            lax.fori_loop(pad_lo_ref[e], pad_hi_ref[e], got, 0)

    def issue(g, c):
        t0 = g * SUBLANES
        for s in range(SUBLANES):
            for p_ref in (p0_ref, p1_ref):
                _row_copy(x_ref.at[g], s, xs_out, p_ref[t0 + s], sem).start()
        return c

    lax.fori_loop(0, tm // SUBLANES, issue, 0)
    _rows_wait(xs_out, 2 * tm, sem)


def _dispatch(pad_lo, pad_hi, pos0, pos1, x, n_rows, tm):
    t = x.shape[0]
    idx = pl.BlockSpec((tm,), lambda i, lo, hi: (i,), memory_space=pltpu.SMEM)
    return pl.pallas_call(
        functools.partial(_dispatch_body, tm=tm),
        grid_spec=pltpu.PrefetchScalarGridSpec(
            num_scalar_prefetch=2,
            grid=(t // tm,),
            in_specs=[idx, idx, pl.BlockSpec((tm // SUBLANES, SUBLANES, D_MODEL), lambda i, lo, hi: (i, 0, 0))],
            out_specs=pl.BlockSpec(memory_space=pl.ANY),
            scratch_shapes=[pltpu.VMEM((SUBLANES, D_MODEL), F32),
                            pltpu.SemaphoreType.DMA(()), pltpu.SemaphoreType.DMA(())]),
        out_shape=jax.ShapeDtypeStruct((n_rows, D_MODEL), F32),
        compiler_params=pltpu.CompilerParams(dimension_semantics=("arbitrary",), vmem_limit_bytes=VMEM_LIMIT,
                                             has_side_effects=True, disable_bounds_checks=True),
        name="moe_dispatch",
    )(pad_lo, pad_hi, pos0, pos1, x.reshape(t // SUBLANES, SUBLANES, D_MODEL))


def _ffn_body(te_ref, nu_ref, xs_ref, wgu_ref, wdn_ref, ys_ref):
    used = pl.program_id(0) < nu_ref[0]

    @pl.when(used)
    def _():
        h = _dot(xs_ref[...].astype(BF16), wgu_ref[0])
        hh = (jax.nn.silu(h[:, :D_EXPERT]) * h[:, D_EXPERT:]).astype(BF16)
        ys_ref[...] = _dot(hh, wdn_ref[0])

    @pl.when(jnp.logical_not(used))
    def _():
        ys_ref[...] = jnp.zeros_like(ys_ref)


def _ffn(tile_expert, n_used, xs, w_gu, w_down, tm):
    p = xs.shape[0]
    row_tile = pl.BlockSpec((tm, D_MODEL), lambda i, te, nu: (i, 0))
    return pl.pallas_call(
        _ffn_body,
        grid_spec=pltpu.PrefetchScalarGridSpec(
            num_scalar_prefetch=2,
            grid=(p // tm,),
            in_specs=[row_tile,
                      pl.BlockSpec((1, D_MODEL, 2 * D_EXPERT), lambda i, te, nu: (te[i], 0, 0)),
                      pl.BlockSpec((1, D_EXPERT, D_MODEL), lambda i, te, nu: (te[i], 0, 0))],
            out_specs=row_tile),
        out_shape=jax.ShapeDtypeStruct((p, D_MODEL), F32),
        compiler_params=_cparams(("arbitrary",)),
        name="moe_expert_ffn",
    )(tile_expert, n_used, xs, w_gu, w_down)


def _combine_body(p0_ref, p1_ref, p0n_ref, p1n_ref, ys_hbm, wb0_ref, wb1_ref, x_ref, g_ref, b_ref, *rest,
                  tm, n_t, n_first):
    o_refs = rest[:-6]
    a0, a1, b0, b1, sem_a, sem_b = rest[-6:]
    i = pl.program_id(0)

    def gather(p_refs, bufs, sem):
        def issue(g, c):
            t0 = g * SUBLANES
            for s in range(SUBLANES):
                for k in range(2):
                    _row_copy(ys_hbm, p_refs[k][t0 + s], bufs[k].at[g], s, sem).start()
            return c

        lax.fori_loop(0, tm // SUBLANES, issue, 0)

    def reduce(bufs, sem):
        _rows_wait(ys_hbm, 2 * tm, sem)
        reps = D_MODEL // LANES
        w0 = jnp.concatenate([wb0_ref[...]] * reps, axis=1)
        w1 = jnp.concatenate([wb1_ref[...]] * reps, axis=1)
        y0 = bufs[0][...].reshape(tm, D_MODEL)
        y1 = bufs[1][...].reshape(tm, D_MODEL)
        out = _ln(DEEPNORM_ALPHA * x_ref[...] + (w0 * y0 + w1 * y1), g_ref[...], b_ref[...])
        if len(o_refs) == 1:
            o_refs[0][...] = out
        else:
            @pl.when(i < n_first)
            def _():
                o_refs[0][...] = out

            @pl.when(i >= n_first)
            def _():
                o_refs[1][...] = out

    slots = (((a0, a1), sem_a), ((b0, b1), sem_b))

    @pl.when(i == 0)
    def _():
        gather((p0_ref, p1_ref), *slots[0])

    for s in range(2):
        @pl.when(i % 2 == s)
        def _():
            @pl.when(i + 1 < n_t)
            def _():
                gather((p0n_ref, p1n_ref), *slots[1 - s])

            reduce(*slots[s])


def _combine(pos0, pos1, ys, wb0, wb1, x, g, b, tm, split_rows=None):
    t = x.shape[0]
    n_t = t // tm
    out_rows = (t,) if split_rows is None else (split_rows, t - split_rows)
    outs = [jax.ShapeDtypeStruct((r, D_MODEL), F32) for r in out_rows]
    tile = pl.BlockSpec((tm, D_MODEL), lambda i: (i, 0))
    wtile = pl.BlockSpec((tm, LANES), lambda i: (i, 0))
    vec = pl.BlockSpec((1, D_MODEL), lambda i: (0, 0))
    idx = pl.BlockSpec((tm,), lambda i: (i,), memory_space=pltpu.SMEM)
    idx_next = pl.BlockSpec((tm,), lambda i: (jnp.minimum(i + 1, n_t - 1),), memory_space=pltpu.SMEM)
    rows = pltpu.VMEM((tm // SUBLANES, SUBLANES, D_MODEL), F32)
    res = pl.pallas_call(
        functools.partial(_combine_body, tm=tm, n_t=n_t, n_first=out_rows[0] // tm),
        grid=(n_t,),
        in_specs=[idx, idx, idx_next, idx_next, pl.BlockSpec(memory_space=pl.ANY), wtile, wtile, tile, vec, vec],
        out_specs=_tok_specs(outs, tm),
        scratch_shapes=[rows, rows, rows, rows, pltpu.SemaphoreType.DMA(()), pltpu.SemaphoreType.DMA(())],
        out_shape=outs,
        compiler_params=pltpu.CompilerParams(dimension_semantics=("arbitrary",), vmem_limit_bytes=VMEM_LIMIT,
                                             disable_bounds_checks=True),
        name="moe_combine",
    )(pos0, pos1, pos0, pos1, ys, wb0, wb1, x, g, b)
    return res[0] if split_rows is None else tuple(res)


def _moe_layer(x, rwh, rwl, rb, tri, w_gu, w_down, g, b, tm, tme, split_rows=None):
    t = x.shape[0]
    n_tiles = (2 * t) // tme + N_EXPERTS
    ert, wb0, wb1, cnt = _router(x, rwh, rwl, rb, tri, tm)
    counts = cnt[:, 0].astype(I32)
    padded = ((counts + tme - 1) // tme) * tme
    ends = jnp.cumsum(padded).astype(I32)
    off = ends - padded
    n_used = ends[-1:] // tme
    tile_start = jnp.arange(n_tiles, dtype=I32) * tme
    tile_expert = jnp.minimum(jnp.sum(tile_start[:, None] >= ends[None, :], axis=1), N_EXPERTS - 1).astype(I32)
    is_e = ert[0:2, :, None] == jnp.arange(N_EXPERTS, dtype=I32)
    pos = jnp.sum(jnp.where(is_e, off, 0), axis=-1) + ert[2:4]
    pad_hi = ends.at[N_EXPERTS - 1].set(n_tiles * tme)
    xs = _dispatch(off + counts, pad_hi, pos[0], pos[1], x, n_tiles * tme, tm)
    ys = _ffn(tile_expert, n_used, xs, w_gu, w_down, tme)
    return _combine(pos[0], pos[1], ys, wb0, wb1, x, g, b, tm, split_rows)


def _trunk(xs, seq_lens, tm, tme, tt, ln_g, ln_b, a_w_in, a_vn_g, a_vn_b, a_w_s, a_b_s, a_w_out,
           b_w_in, b_conv_w, b_conv_b, b_w_gates, b_b_gates, b_lambda, b_w_out,
           router_w, router_b, moe_w_gu, moe_w_down):
    row = lambda v: v.reshape(1, -1).astype(F32)
    rw = jnp.pad(router_w.astype(F32), ((0, 0), (0, LANES - N_EXPERTS)))
    rwh = rw.astype(BF16)
    rwl = (rw - rwh.astype(F32)).astype(BF16)
    rb = jnp.pad(router_b.astype(F32), (0, LANES - N_EXPERTS)).reshape(1, LANES)
    tri = (jnp.arange(tm)[:, None] < jnp.arange(tm)[None, :]).astype(BF16)
    ia = ib = 0
    x = None
    for layer in range(DEPTH):
        xin = xs if x is None else (x,)
        last = layer == DEPTH - 1
        g0, b0 = row(ln_g[layer, 0]), row(ln_b[layer, 0])
        if layer % 2 == 0:
            bs_full = jnp.repeat(a_b_s[ia].T.astype(F32), CHUNK, axis=1)
            y = _a1(xin, a_w_in[ia].astype(BF16), row(a_vn_g[ia]), row(a_vn_b[ia]), a_w_s[ia].astype(BF16), bs_full, tm)
            x = _out_proj(_op_a_body, [y], a_w_out[ia].astype(BF16), xin, g0, b0, tm, "op_a")
            ia += 1
        else:
            gg, xr = _b1(x, b_w_in[ib].astype(BF16), tm)
            wgt = (0.5 * b_w_gates[ib]).astype(BF16)
            wg = jnp.stack([jnp.concatenate([wgt[0], wgt[1]], axis=-1),
                            jnp.concatenate([wgt[2], wgt[3]], axis=-1)])
            hf, hb = _b2(xr, b_conv_w[ib].astype(F32), row(b_conv_b[ib]), wg, 0.5 * b_b_gates[ib].astype(F32),
                         b_lambda[ib].astype(F32), seq_lens, tt)
            x = _out_proj(_op_b_body, [hf, hb, gg], b_w_out[ib].astype(BF16), (x,), g0, b0, tm, "op_b")
            ib += 1
        x = _moe_layer(x, rwh, rwl, rb, tri, moe_w_gu[layer].astype(BF16), moe_w_down[layer].astype(BF16),
                       row(ln_g[layer, 1]), row(ln_b[layer, 1]), tm, tme,
                       split_rows=xs[0].shape[0] if last else None)
    return x


def kernel(x_prompt, x_sample, ln_g, ln_b, a_w_in, a_vn_g, a_vn_b, a_w_s, a_b_s, a_w_out, b_w_in, b_conv_w, b_conv_b,
           b_w_gates, b_b_gates, b_lambda, b_w_out, router_w, router_b, moe_w_gu, moe_w_down):
    d = x_prompt.shape[-1]
    seq_lens = (x_prompt.shape[1],) * x_prompt.shape[0] + (x_sample.shape[1],) * x_sample.shape[0]
    xs = (x_prompt.reshape(-1, d), x_sample.reshape(-1, d))
    y_p, y_s = _trunk(xs, seq_lens, TM, TME, TT, ln_g, ln_b, a_w_in, a_vn_g, a_vn_b, a_w_s, a_b_s, a_w_out,
                      b_w_in, b_conv_w, b_conv_b, b_w_gates, b_b_gates, b_lambda, b_w_out,
                      router_w, router_b, moe_w_gu, moe_w_down)
    return y_p.reshape(x_prompt.shape), y_s.reshape(x_sample.shape)
```

```python
import functools

import jax
import jax.numpy as jnp
from jax import lax
from jax.experimental import pallas as pl
from jax.experimental.pallas import tpu as pltpu

F32 = jnp.float32
BF16 = jnp.bfloat16
I32 = jnp.int32
U32 = jnp.uint32

D_MODEL = 2048
DEPTH = 4
CHUNK = 128
A_HALF = D_MODEL
A_GROUPS = 16
RNN_WIDTH = D_MODEL
RNN_BLOCKS = 16
RNN_BDIM = RNN_WIDTH // RNN_BLOCKS
LRU_C = 8.0
N_EXPERTS = 16
N_GROUPS = 4
EXPERTS_PER_GROUP = N_EXPERTS // N_GROUPS
D_EXPERT = D_MODEL // 2
DEEPNORM_ALPHA = (2.0 * DEPTH) ** 0.25
LN_EPS = 1e-5

LANES = 128
SUBLANES = 8
VMEM_LIMIT = 56 * 1024 * 1024

TM = 512
TME = 512
TT = 256


def _cparams(sem):
    return pltpu.CompilerParams(dimension_semantics=sem, vmem_limit_bytes=VMEM_LIMIT)


def _resident(shape):
    nd = len(shape)
    return pl.BlockSpec(shape, lambda *_: (0,) * nd, pipeline_mode=pl.Buffered(1))


def _dot(a, b):
    return jnp.dot(a, b, preferred_element_type=F32)


def _ln(x, g, b):
    mu = jnp.mean(x, axis=-1, keepdims=True)
    xc = x - mu
    var = jnp.mean(xc * xc, axis=-1, keepdims=True)
    return xc * lax.rsqrt(var + LN_EPS) * g + b


def _tok_specs(pieces, tm):
    if len(pieces) == 1:
        return [pl.BlockSpec((tm, D_MODEL), lambda i: (i, 0))]
    n_first = pieces[0].shape[0] // tm
    return [pl.BlockSpec((tm, D_MODEL), lambda i: (jnp.minimum(i, n_first - 1), 0)),
            pl.BlockSpec((tm, D_MODEL), lambda i: (jnp.maximum(i - n_first, 0), 0))]


def _tok_tile(refs, n_first):
    if len(refs) == 1:
        return refs[0][...]
    return jnp.where(pl.program_id(0) < n_first, refs[0][...], refs[1][...])


def _a1_body(*refs, tm, n_x, n_first):
    w_ref, vg_ref, vb_ref, ws_ref, bs_ref, y_ref, s_scr = refs[n_x:]
    xb = _tok_tile(refs[:n_x], n_first).astype(BF16)
    v = jax.nn.gelu(_dot(xb, w_ref[:, A_HALF:]))
    v = _ln(v, vg_ref[...], vb_ref[...]).astype(BF16)
    for c in range(tm // CHUNK):
        rows = slice(c * CHUNK, (c + 1) * CHUNK)
        for g in range(A_GROUPS):
            cols = slice(g * LANES, (g + 1) * LANES)
            s_scr[rows, cols] = _dot(ws_ref[g], v[rows, cols]) + bs_ref[:, cols]
    u = jax.nn.gelu(_dot(xb, w_ref[:, :A_HALF]))
    y_ref[...] = (u * s_scr[...]).astype(BF16)


def _a1(xs, w_in, vn_g, vn_b, w_s, bs_full, tm):
    t = sum(x.shape[0] for x in xs)
    return pl.pallas_call(
        functools.partial(_a1_body, tm=tm, n_x=len(xs), n_first=xs[0].shape[0] // tm),
        grid=(t // tm,),
        in_specs=_tok_specs(xs, tm) + [
            _resident((D_MODEL, 2 * A_HALF)),
            _resident((1, A_HALF)),
            _resident((1, A_HALF)),
            _resident((A_GROUPS, CHUNK, CHUNK)),
            _resident((CHUNK, A_HALF)),
        ],
        out_specs=pl.BlockSpec((tm, A_HALF), lambda i: (i, 0)),
        out_shape=jax.ShapeDtypeStruct((t, A_HALF), BF16),
        scratch_shapes=[pltpu.VMEM((tm, A_HALF), F32)],
        compiler_params=_cparams(("arbitrary",)),
        name="a1_gmlp_front",
    )(*xs, w_in, vn_g, vn_b, w_s, bs_full)


def _op_a_body(*refs, n_x, n_first):
    y_ref, w_ref = refs[:2]
    g_ref, b_ref, o_ref = refs[2 + n_x:]
    m = _dot(y_ref[...], w_ref[...])
    o_ref[...] = _ln(DEEPNORM_ALPHA * _tok_tile(refs[2:2 + n_x], n_first) + m, g_ref[...], b_ref[...])


def _op_b_body(*refs, n_x, n_first):
    hf_ref, hb_ref, gg_ref, w_ref = refs[:4]
    g_ref, b_ref, o_ref = refs[4 + n_x:]
    y = ((hf_ref[...] + hb_ref[...]) * gg_ref[...].astype(F32)).astype(BF16)
    m = _dot(y, w_ref[...])
    o_ref[...] = _ln(DEEPNORM_ALPHA * _tok_tile(refs[4:4 + n_x], n_first) + m, g_ref[...], b_ref[...])


def _out_proj(body, acts, w_out, xs, g, b, tm, name):
    t = sum(x.shape[0] for x in xs)
    tile = pl.BlockSpec((tm, D_MODEL), lambda i: (i, 0))
    return pl.pallas_call(
        functools.partial(body, n_x=len(xs), n_first=xs[0].shape[0] // tm),
        grid=(t // tm,),
        in_specs=([tile] * len(acts) + [_resident(w_out.shape)] + _tok_specs(xs, tm)
                  + [_resident((1, D_MODEL)), _resident((1, D_MODEL))]),
        out_specs=tile,
        out_shape=jax.ShapeDtypeStruct((t, D_MODEL), F32),
        compiler_params=_cparams(("arbitrary",)),
        name=name,
    )(*acts, w_out, *xs, g, b)


def _b1_body(x_ref, w_ref, gg_ref, xr_ref):
    xb = x_ref[...].astype(BF16)
    gg_ref[...] = jax.nn.gelu(_dot(xb, w_ref[:, :RNN_WIDTH])).astype(BF16)
    xr_ref[...] = _dot(xb, w_ref[:, RNN_WIDTH:])


def _b1(x, w_in, tm):
    t = x.shape[0]
    tile = pl.BlockSpec((tm, D_MODEL), lambda i: (i, 0))
    return pl.pallas_call(
        _b1_body,
        grid=(t // tm,),
        in_specs=[tile, _resident((D_MODEL, 2 * RNN_WIDTH))],
        out_specs=[tile, tile],
        out_shape=[jax.ShapeDtypeStruct((t, RNN_WIDTH), BF16), jax.ShapeDtypeStruct((t, RNN_WIDTH), F32)],
        compiler_params=_cparams(("arbitrary",)),
        name="b1_rglru_front",
    )(x, w_in)


def _scan_rows(a_s, b_s, o_ref, carry_scr, *, tt, reverse):
    n_grp = tt // SUBLANES
    width = 512
    row_id = lax.broadcasted_iota(I32, (SUBLANES, width), 0)

    def step(k, carry):
        g = (n_grp - 1 - k) if reverse else k
        r0 = pl.multiple_of(g * SUBLANES, SUBLANES)
        new = []
        for q in range(RNN_WIDTH // width):
            cols = slice(q * width, (q + 1) * width)
            a = a_s[pl.ds(r0, SUBLANES), cols]
            b = b_s[pl.ds(r0, SUBLANES), cols]
            for d in (1, 2, 4):
                if reverse:
                    keep = row_id < SUBLANES - d
                    sh = SUBLANES - d
                else:
                    keep = row_id >= d
                    sh = d
                a_sh = jnp.where(keep, pltpu.roll(a, sh, axis=0), 1.0)
                b_sh = jnp.where(keep, pltpu.roll(b, sh, axis=0), 0.0)
                b = a * b_sh + b
                a = a * a_sh
            h = a * carry[q] + b
            o_ref[pl.ds(r0, SUBLANES), cols] = h
            edge = h[0:1, :] if reverse else h[SUBLANES - 1:SUBLANES, :]
            new.append(jnp.broadcast_to(edge, (SUBLANES, width)))
        return tuple(new)

    init = tuple(carry_scr[:, q * width:(q + 1) * width] for q in range(RNN_WIDTH // width))
    fin = lax.fori_loop(0, n_grp, step, init)
    for q in range(RNN_WIDTH // width):
        carry_scr[:, q * width:(q + 1) * width] = fin[q]


def _b2_direction(d, x_ref, prev_ref, next_ref, at_start, at_end, reset, cw_ref, cb_ref, wg_ref, bg_ref, sp_ref,
                  o_ref, carry_scr, xc_s, a_s, b_s, *, tt):
    x = x_ref[...]
    prev = jnp.where(at_start, 0.0, prev_ref[...])
    nxt = jnp.where(at_end, 0.0, next_ref[...])
    rid = lax.broadcasted_iota(I32, (SUBLANES, 1), 0)
    last = tt - SUBLANES
    xm1 = pltpu.roll(x, 1, axis=0)
    xm1 = jnp.concatenate([jnp.where(rid == 0, prev[SUBLANES - 1:SUBLANES, :], xm1[0:SUBLANES, :]), xm1[SUBLANES:, :]], axis=0)
    xp1 = pltpu.roll(x, tt - 1, axis=0)
    xp1 = jnp.concatenate([xp1[:last, :], jnp.where(rid == SUBLANES - 1, nxt[0:1, :], xp1[last:, :])], axis=0)
    xp2 = pltpu.roll(x, tt - 2, axis=0)
    tail = jnp.where(rid == SUBLANES - 2, nxt[0:1, :], jnp.where(rid == SUBLANES - 1, nxt[1:2, :], xp2[last:, :]))
    xp2 = jnp.concatenate([xp2[:last, :], tail], axis=0)
    xc_s[...] = (cb_ref[...] + cw_ref[0:1, :] * xm1 + cw_ref[1:2, :] * x
                 + cw_ref[2:3, :] * xp1 + cw_ref[3:4, :] * xp2)
    for h in range(RNN_BLOCKS):
        cols = slice(h * RNN_BDIM, (h + 1) * RNN_BDIM)
        xh = xc_s[:, cols]
        g2 = _dot(xh.astype(BF16), wg_ref[d, h])
        t_r = jnp.tanh(g2[:, :RNN_BDIM] + bg_ref[2 * d:2 * d + 1, cols])
        t_i = jnp.tanh(g2[:, RNN_BDIM:] + bg_ref[2 * d + 1:2 * d + 2, cols])
        c2 = sp_ref[d:d + 1, cols]
        log_a = c2 * t_r + c2
        a = jnp.exp(log_a)
        mult = jnp.sqrt(-jnp.tanh(log_a) * (a * a + 1.0))
        a_s[:, cols] = a
        b_s[:, cols] = mult * ((0.5 * t_i + 0.5) * xh)

    @pl.when(reset)
    def _():
        carry_scr[...] = jnp.zeros_like(carry_scr)

    _scan_rows(a_s, b_s, o_ref, carry_scr, tt=tt, reverse=(d == 1))


def _b2_body(xf_ref, pf_ref, nf_ref, xb_ref, pb_ref, nb_ref, cw_ref, cb_ref, wg_ref, bg_ref, lam_ref,
             hf_ref, hb_ref, cf_scr, cbk_scr, sp_scr, xc_s, a_s, b_s, *, tt, n_t, starts, ends):
    i = pl.program_id(0)
    j = n_t - 1 - i

    def hits(idx, marks):
        r = idx == marks[0]
        for m in marks[1:]:
            r = jnp.logical_or(r, idx == m)
        return r

    sp_scr[...] = (-0.5 * LRU_C) * jax.nn.softplus(-lam_ref[...])
    common = dict(cw_ref=cw_ref, cb_ref=cb_ref, wg_ref=wg_ref, bg_ref=bg_ref, sp_ref=sp_scr,
                  xc_s=xc_s, a_s=a_s, b_s=b_s, tt=tt)
    f_start, f_end = hits(i, starts), hits(i, ends)
    _b2_direction(0, xf_ref, pf_ref, nf_ref, f_start, f_end, f_start, o_ref=hf_ref, carry_scr=cf_scr, **common)
    b_start, b_end = hits(j, starts), hits(j, ends)
    _b2_direction(1, xb_ref, pb_ref, nb_ref, b_start, b_end, b_end, o_ref=hb_ref, carry_scr=cbk_scr, **common)


def _b2(xr, conv_w, conv_b, wg, b_gates, lam, seq_lens, tt):
    t = xr.shape[0]
    n_t = t // tt
    hp = tt // SUBLANES
    n_h = t // SUBLANES
    bounds = [0]
    for s in seq_lens:
        bounds.append(bounds[-1] + s)
    starts = tuple(b // tt for b in bounds[:-1])
    ends = tuple(b // tt - 1 for b in bounds[1:])
    w = RNN_WIDTH
    main_f = pl.BlockSpec((tt, w), lambda i: (i, 0))
    prev_f = pl.BlockSpec((SUBLANES, w), lambda i: (jnp.maximum(i * hp - 1, 0), 0))
    next_f = pl.BlockSpec((SUBLANES, w), lambda i: (jnp.minimum((i + 1) * hp, n_h - 1), 0))
    main_b = pl.BlockSpec((tt, w), lambda i: (n_t - 1 - i, 0))
    prev_b = pl.BlockSpec((SUBLANES, w), lambda i: (jnp.maximum((n_t - 1 - i) * hp - 1, 0), 0))
    next_b = pl.BlockSpec((SUBLANES, w), lambda i: (jnp.minimum((n_t - i) * hp, n_h - 1), 0))
    return pl.pallas_call(
        functools.partial(_b2_body, tt=tt, n_t=n_t, starts=starts, ends=ends),
        grid=(n_t,),
        in_specs=[main_f, prev_f, next_f, main_b, prev_b, next_b,
                  _resident((4, w)), _resident((1, w)), _resident(wg.shape), _resident((4, w)), _resident((2, w))],
        out_specs=[main_f, main_b],
        out_shape=[jax.ShapeDtypeStruct((t, w), F32), jax.ShapeDtypeStruct((t, w), F32)],
        scratch_shapes=[pltpu.VMEM((SUBLANES, w), F32), pltpu.VMEM((SUBLANES, w), F32), pltpu.VMEM((2, w), F32),
                        pltpu.VMEM((tt, w), F32), pltpu.VMEM((tt, w), F32), pltpu.VMEM((tt, w), F32)],
        compiler_params=_cparams(("arbitrary",)),
        name="b2_conv_gates_scan",
    )(xr, xr, xr, xr, xr, xr, conv_w, conv_b, wg, b_gates, lam)


def _top2_of4(a):
    m1 = jnp.maximum(jnp.maximum(a[0], a[1]), jnp.maximum(a[2], a[3]))
    i1 = jnp.where(a[0] == m1, 0, jnp.where(a[1] == m1, 1, jnp.where(a[2] == m1, 2, 3)))
    b = [jnp.where(i1 == k, -1.0, a[k]) for k in range(4)]
    m2 = jnp.maximum(jnp.maximum(b[0], b[1]), jnp.maximum(b[2], b[3]))
    i2 = jnp.where(b[0] == m2, 0, jnp.where(b[1] == m2, 1, jnp.where(b[2] == m2, 2, 3)))
    return m1, i1, m2, i2


def _router_body(x_ref, wh_ref, wl_ref, rb_ref, tri_ref, ert_ref, wb0_ref, wb1_ref, cnt_ref, base_scr, *, tm):
    @pl.when(pl.program_id(0) == 0)
    def _():
        base_scr[...] = jnp.zeros_like(base_scr)

    x = x_ref[...]
    xh = x.astype(BF16)
    xl = (x - xh.astype(F32)).astype(BF16)
    wh = wh_ref[...]
    logits = _dot(xh, wh) + _dot(xl, wh) + _dot(xh, wl_ref[...]) + rb_ref[...]
    lt = jnp.transpose(logits)
    l = [lt[e:e + 1, :] for e in range(N_EXPERTS)]
    mx = l[0]
    for e in range(1, N_EXPERTS):
        mx = jnp.maximum(mx, l[e])
    ex = [jnp.exp(v - mx) for v in l]
    den = ex[0]
    for e in range(1, N_EXPERTS):
        den = den + ex[e]
    p = [v / den for v in ex]

    tops = [_top2_of4(p[g * EXPERTS_PER_GROUP:(g + 1) * EXPERTS_PER_GROUP]) for g in range(N_GROUPS)]
    score = [t[0] + t[2] for t in tops]
    best = jnp.maximum(jnp.maximum(score[0], score[1]), jnp.maximum(score[2], score[3]))
    gsel = jnp.where(score[0] == best, 0, jnp.where(score[1] == best, 1, jnp.where(score[2] == best, 2, 3)))

    def pick(field):
        return jnp.where(gsel == 0, tops[0][field],
                         jnp.where(gsel == 1, tops[1][field], jnp.where(gsel == 2, tops[2][field], tops[3][field])))

    v1, i1, v2, i2 = pick(0), pick(1), pick(2), pick(3)
    e1 = gsel * EXPERTS_PER_GROUP + i1
    e2 = gsel * EXPERTS_PER_GROUP + i2
    vs = v1 + v2
    w1 = v1 / vs
    w2 = v2 / vs

    eid = lax.broadcasted_iota(I32, (N_EXPERTS, tm), 0)
    hit1 = eid == e1
    hit2 = eid == e2
    oh = jnp.where(hit1, 1.0, 0.0) + jnp.where(hit2, 1.0, 0.0)
    before = _dot(oh.astype(BF16), tri_ref[...])
    tot = before + jnp.concatenate([base_scr[...]] * (tm // LANES), axis=1)
    r1 = jnp.sum(jnp.where(hit1, tot, 0.0), axis=0, keepdims=True)
    r2 = jnp.sum(jnp.where(hit2, tot, 0.0), axis=0, keepdims=True)
    base_scr[...] = base_scr[...] + jnp.sum(oh, axis=1, keepdims=True)
    cnt_ref[...] = base_scr[...]

    zero = jnp.zeros((1, tm), I32)
    ert_ref[...] = jnp.concatenate([e1, e2, r1.astype(I32), r2.astype(I32), zero, zero, zero, zero], axis=0)
    wb0_ref[...] = jnp.transpose(jnp.broadcast_to(w1, (LANES, tm)))
    wb1_ref[...] = jnp.transpose(jnp.broadcast_to(w2, (LANES, tm)))


def _router(x, rwh, rwl, rb, tri, tm):
    t = x.shape[0]
    return pl.pallas_call(
        functools.partial(_router_body, tm=tm),
        grid=(t // tm,),
        in_specs=[pl.BlockSpec((tm, D_MODEL), lambda i: (i, 0)),
                  _resident((D_MODEL, LANES)), _resident((D_MODEL, LANES)), _resident((1, LANES)), _resident((tm, tm))],
        out_specs=[pl.BlockSpec((SUBLANES, tm), lambda i: (0, i)),
                   pl.BlockSpec((tm, LANES), lambda i: (i, 0)),
                   pl.BlockSpec((tm, LANES), lambda i: (i, 0)),
                   pl.BlockSpec((N_EXPERTS, LANES), lambda i: (0, 0))],
        out_shape=[jax.ShapeDtypeStruct((SUBLANES, t), I32),
                   jax.ShapeDtypeStruct((t, LANES), F32),
                   jax.ShapeDtypeStruct((t, LANES), F32),
                   jax.ShapeDtypeStruct((N_EXPERTS, LANES), F32)],
        scratch_shapes=[pltpu.VMEM((N_EXPERTS, LANES), F32)],
        compiler_params=_cparams(("arbitrary",)),
        name="moe_router",
    )(x, rwh, rwl, rb, tri)


HALF = D_MODEL // 2
ROW_TILE = (SUBLANES, LANES)


def _pack_rows(x):
    bits = lax.bitcast_convert_type(x.astype(BF16).astype(F32), U32)
    return (bits[:, :HALF] >> 16) | (bits[:, HALF:] & jnp.uint32(0xFFFF0000))


def _unpack_rows(w):
    lo = lax.bitcast_convert_type(w << 16, F32)
    hi = lax.bitcast_convert_type(w & jnp.uint32(0xFFFF0000), F32)
    return jnp.concatenate([lo, hi], axis=1)


def _rows_to_tiles(pk_ref, tiles_ref, n):
    def body(g, c):
        r0 = pl.multiple_of(g * SUBLANES, SUBLANES)
        for k in range(SUBLANES):
            tiles_ref[pl.ds(g * SUBLANES * SUBLANES + k, SUBLANES, stride=SUBLANES), :] = (
                pk_ref[pl.ds(r0, SUBLANES), k * LANES:(k + 1) * LANES])
        return c

    lax.fori_loop(0, n // SUBLANES, body, 0)


def _tiles_to_rows(tiles_ref, pk_ref, n):
    def body(g, c):
        r0 = pl.multiple_of(g * SUBLANES, SUBLANES)
        for k in range(SUBLANES):
            pk_ref[pl.ds(r0, SUBLANES), k * LANES:(k + 1) * LANES] = (
                tiles_ref[pl.ds(g * SUBLANES * SUBLANES + k, SUBLANES, stride=SUBLANES), :])
        return c

    lax.fori_loop(0, n // SUBLANES, body, 0)


def _tile_copy(src, s, dst, d, sem):
    pick = lambda ref, r: ref.at[r] if len(ref.shape) == 3 else ref.at[pl.ds(r * SUBLANES, SUBLANES), :]
    return pltpu.make_async_copy(pick(src, s), pick(dst, d), sem)


def _tiles_wait(hbm, n, sem):
    pltpu.make_async_copy(hbm.at[pl.ds(0, n)], hbm.at[pl.ds(0, n)], sem).wait()


def _dispatch_body(pad_lo_ref, pad_hi_ref, p0_ref, p1_ref, x_ref, xs_out, pk, tiles, zrow, sem, zsem, *, tm):
    @pl.when(pl.program_id(0) == 0)
    def _():
        zrow[...] = jnp.zeros_like(zrow)
        for e in range(N_EXPERTS):
            def put(r, c):
                _tile_copy(zrow, 0, xs_out, r, zsem).start()
                return c

            lax.fori_loop(pad_lo_ref[e], pad_hi_ref[e], put, 0)
        for e in range(N_EXPERTS):
            def got(r, c):
                _tile_copy(zrow, 0, xs_out, 0, zsem).wait()
                return c

            lax.fori_loop(pad_lo_ref[e], pad_hi_ref[e], got, 0)

    pk[...] = _pack_rows(x_ref[...])
    _rows_to_tiles(pk, tiles, tm)

    def issue(g, c):
        t0 = g * SUBLANES
        for s in range(SUBLANES):
            for p_ref in (p0_ref, p1_ref):
                _tile_copy(tiles, t0 + s, xs_out, p_ref[t0 + s], sem).start()
        return c

    lax.fori_loop(0, tm // SUBLANES, issue, 0)
    _tiles_wait(xs_out, 2 * tm, sem)


def _dispatch(pad_lo, pad_hi, pos0, pos1, x, n_rows, tm):
    t = x.shape[0]
    idx = pl.BlockSpec((tm,), lambda i, lo, hi: (i,), memory_space=pltpu.SMEM)
    return pl.pallas_call(
        functools.partial(_dispatch_body, tm=tm),
        grid_spec=pltpu.PrefetchScalarGridSpec(
            num_scalar_prefetch=2,
            grid=(t // tm,),
            in_specs=[idx, idx, pl.BlockSpec((tm, D_MODEL), lambda i, lo, hi: (i, 0))],
            out_specs=pl.BlockSpec(memory_space=pl.ANY),
            scratch_shapes=[pltpu.VMEM((tm, HALF), U32), pltpu.VMEM((tm * SUBLANES, LANES), U32),
                            pltpu.VMEM(ROW_TILE, U32),
                            pltpu.SemaphoreType.DMA(()), pltpu.SemaphoreType.DMA(())]),
        out_shape=jax.ShapeDtypeStruct((n_rows,) + ROW_TILE, U32),
        compiler_params=pltpu.CompilerParams(dimension_semantics=("arbitrary",), vmem_limit_bytes=VMEM_LIMIT,
                                             has_side_effects=True, disable_bounds_checks=True),
        name="moe_dispatch",
    )(pad_lo, pad_hi, pos0, pos1, x)


def _ffn_body(te_ref, nu_ref, xs_ref, wgu_ref, wdn_ref, ys_ref, pk, *, tm):
    used = pl.program_id(0) < nu_ref[0]

    @pl.when(used)
    def _():
        _tiles_to_rows(xs_ref, pk, tm)
        h = _dot(_unpack_rows(pk[...]).astype(BF16), wgu_ref[0, 0])
        hh = (jax.nn.silu(h[:, :D_EXPERT]) * h[:, D_EXPERT:]).astype(BF16)
        pk[...] = _pack_rows(_dot(hh, wdn_ref[0, 0]))
        _rows_to_tiles(pk, ys_ref, tm)

    @pl.when(jnp.logical_not(used))
    def _():
        ys_ref[...] = jnp.zeros_like(ys_ref)


def _ffn(tile_expert, n_used, xs, w_gu, w_down, layer, tm):
    p = xs.shape[0]
    flat = xs.reshape(p * SUBLANES, LANES)
    row_tile = pl.BlockSpec((tm * SUBLANES, LANES), lambda i, te, nu: (i, 0))
    ys = pl.pallas_call(
        functools.partial(_ffn_body, tm=tm),
        grid_spec=pltpu.PrefetchScalarGridSpec(
            num_scalar_prefetch=2,
            grid=(p // tm,),
            in_specs=[row_tile,
                      pl.BlockSpec((1, 1, D_MODEL, 2 * D_EXPERT), lambda i, te, nu: (layer, te[i], 0, 0)),
                      pl.BlockSpec((1, 1, D_EXPERT, D_MODEL), lambda i, te, nu: (layer, te[i], 0, 0))],
            out_specs=row_tile,
            scratch_shapes=[pltpu.VMEM((tm, HALF), U32)]),
        out_shape=jax.ShapeDtypeStruct(flat.shape, U32),
        compiler_params=_cparams(("arbitrary",)),
        name="moe_expert_ffn",
    )(tile_expert, n_used, flat, w_gu, w_down)
    return ys.reshape(xs.shape)


def _combine_body(p0_ref, p1_ref, p0n_ref, p1n_ref, ys_hbm, wb0_ref, wb1_ref, x_ref, g_ref, b_ref, *rest,
                  tm, n_t, n_first):
    o_refs = rest[:-8]
    a0, a1, b0, b1, pk0, pk1, sem_a, sem_b = rest[-8:]
    i = pl.program_id(0)

    def gather(p_refs, bufs, sem):
        def issue(g, c):
            t0 = g * SUBLANES
            for s in range(SUBLANES):
                for k in range(2):
                    _tile_copy(ys_hbm, p_refs[k][t0 + s], bufs[k], t0 + s, sem).start()
            return c

        lax.fori_loop(0, tm // SUBLANES, issue, 0)

    def reduce(bufs, sem):
        _tiles_wait(ys_hbm, 2 * tm, sem)
        _tiles_to_rows(bufs[0], pk0, tm)
        _tiles_to_rows(bufs[1], pk1, tm)
        reps = D_MODEL // LANES
        w0 = jnp.concatenate([wb0_ref[...]] * reps, axis=1)
        w1 = jnp.concatenate([wb1_ref[...]] * reps, axis=1)
        f = w0 * _unpack_rows(pk0[...]) + w1 * _unpack_rows(pk1[...])
        out = _ln(DEEPNORM_ALPHA * x_ref[...] + f, g_ref[...], b_ref[...])
        if len(o_refs) == 1:
            o_refs[0][...] = out
        else:
            @pl.when(i < n_first)
            def _():
                o_refs[0][...] = out

            @pl.when(i >= n_first)
            def _():
                o_refs[1][...] = out

    slots = (((a0, a1), sem_a), ((b0, b1), sem_b))

    @pl.when(i == 0)
    def _():
        gather((p0_ref, p1_ref), *slots[0])

    for s in range(2):
        @pl.when(i % 2 == s)
        def _():
            @pl.when(i + 1 < n_t)
            def _():
                gather((p0n_ref, p1n_ref), *slots[1 - s])

            reduce(*slots[s])


def _combine(pos0, pos1, ys, wb0, wb1, x, g, b, tm, split_rows=None):
    t = x.shape[0]
    n_t = t // tm
    out_rows = (t,) if split_rows is None else (split_rows, t - split_rows)
    outs = [jax.ShapeDtypeStruct((r, D_MODEL), F32) for r in out_rows]
    tile = pl.BlockSpec((tm, D_MODEL), lambda i: (i, 0))
    wtile = pl.BlockSpec((tm, LANES), lambda i: (i, 0))
    vec = pl.BlockSpec((1, D_MODEL), lambda i: (0, 0))
    idx = pl.BlockSpec((tm,), lambda i: (i,), memory_space=pltpu.SMEM)
    idx_next = pl.BlockSpec((tm,), lambda i: (jnp.minimum(i + 1, n_t - 1),), memory_space=pltpu.SMEM)
    rows = pltpu.VMEM((tm * SUBLANES, LANES), U32)
    words = pltpu.VMEM((tm, HALF), U32)
    res = pl.pallas_call(
        functools.partial(_combine_body, tm=tm, n_t=n_t, n_first=out_rows[0] // tm),
        grid=(n_t,),
        in_specs=[idx, idx, idx_next, idx_next, pl.BlockSpec(memory_space=pl.ANY), wtile, wtile, tile, vec, vec],
        out_specs=_tok_specs(outs, tm),
        scratch_shapes=[rows, rows, rows, rows, words, words,
                        pltpu.SemaphoreType.DMA(()), pltpu.SemaphoreType.DMA(())],
        out_shape=outs,
        compiler_params=pltpu.CompilerParams(dimension_semantics=("arbitrary",), vmem_limit_bytes=VMEM_LIMIT,
                                             disable_bounds_checks=True),
        name="moe_combine",
    )(pos0, pos1, pos0, pos1, ys, wb0, wb1, x, g, b)
    return res[0] if split_rows is None else tuple(res)


def _moe_layer(x, rwh, rwl, rb, tri, w_gu, w_down, layer, g, b, tm, tme, split_rows=None):
    t = x.shape[0]
    n_tiles = (2 * t) // tme + N_EXPERTS
    ert, wb0, wb1, cnt = _router(x, rwh, rwl, rb, tri, tm)
    counts = cnt[:, 0].astype(I32)
    padded = ((counts + tme - 1) // tme) * tme
    ends = jnp.cumsum(padded).astype(I32)
    off = ends - padded
    n_used = ends[-1:] // tme
    tile_start = jnp.arange(n_tiles, dtype=I32) * tme
    tile_expert = jnp.minimum(jnp.sum(tile_start[:, None] >= ends[None, :], axis=1), N_EXPERTS - 1).astype(I32)
    is_e = ert[0:2, :, None] == jnp.arange(N_EXPERTS, dtype=I32)
    pos = jnp.sum(jnp.where(is_e, off, 0), axis=-1) + ert[2:4]
    pad_hi = ends.at[N_EXPERTS - 1].set(n_tiles * tme)
    xs = _dispatch(off + counts, pad_hi, pos[0], pos[1], x, n_tiles * tme, tm)
    ys = _ffn(tile_expert, n_used, xs, w_gu, w_down, layer, tme)
    return _combine(pos[0], pos[1], ys, wb0, wb1, x, g, b, tm, split_rows)


def _trunk(xs, seq_lens, tm, tme, tt, ln_g, ln_b, a_w_in, a_vn_g, a_vn_b, a_w_s, a_b_s, a_w_out,
           b_w_in, b_conv_w, b_conv_b, b_w_gates, b_b_gates, b_lambda, b_w_out,
           router_w, router_b, moe_w_gu, moe_w_down):
    row = lambda v: v.reshape(1, -1).astype(F32)
    rw = jnp.pad(router_w.astype(F32), ((0, 0), (0, LANES - N_EXPERTS)))
    rwh = rw.astype(BF16)
    rwl = (rw - rwh.astype(F32)).astype(BF16)
    rb = jnp.pad(router_b.astype(F32), (0, LANES - N_EXPERTS)).reshape(1, LANES)
    tri = (jnp.arange(tm)[:, None] < jnp.arange(tm)[None, :]).astype(BF16)
    w_gu, w_down = moe_w_gu.astype(BF16), moe_w_down.astype(BF16)
    ia = ib = 0
    x = None
    for layer in range(DEPTH):
        xin = xs if x is None else (x,)
        last = layer == DEPTH - 1
        g0, b0 = row(ln_g[layer, 0]), row(ln_b[layer, 0])
        if layer % 2 == 0:
            bs_full = jnp.repeat(a_b_s[ia].T.astype(F32), CHUNK, axis=1)
            y = _a1(xin, a_w_in[ia].astype(BF16), row(a_vn_g[ia]), row(a_vn_b[ia]), a_w_s[ia].astype(BF16), bs_full, tm)
            x = _out_proj(_op_a_body, [y], a_w_out[ia].astype(BF16), xin, g0, b0, tm, "op_a")
            ia += 1
        else:
            gg, xr = _b1(x, b_w_in[ib].astype(BF16), tm)
            wgt = (0.5 * b_w_gates[ib]).astype(BF16)
            wg = jnp.stack([jnp.concatenate([wgt[0], wgt[1]], axis=-1),
                            jnp.concatenate([wgt[2], wgt[3]], axis=-1)])
            hf, hb = _b2(xr, b_conv_w[ib].astype(F32), row(b_conv_b[ib]), wg, 0.5 * b_b_gates[ib].astype(F32),
                         b_lambda[ib].astype(F32), seq_lens, tt)
            x = _out_proj(_op_b_body, [hf, hb, gg], b_w_out[ib].astype(BF16), (x,), g0, b0, tm, "op_b")
            ib += 1
        x = _moe_layer(x, rwh, rwl, rb, tri, w_gu, w_down, layer,
                       row(ln_g[layer, 1]), row(ln_b[layer, 1]), tm, tme,
                       split_rows=xs[0].shape[0] if last else None)
    return x


def kernel(x_prompt, x_sample, ln_g, ln_b, a_w_in, a_vn_g, a_vn_b, a_w_s, a_b_s, a_w_out, b_w_in, b_conv_w, b_conv_b,
           b_w_gates, b_b_gates, b_lambda, b_w_out, router_w, router_b, moe_w_gu, moe_w_down):
    d = x_prompt.shape[-1]
    seq_lens = (x_prompt.shape[1],) * x_prompt.shape[0] + (x_sample.shape[1],) * x_sample.shape[0]
    xs = (x_prompt.reshape(-1, d), x_sample.reshape(-1, d))
    y_p, y_s = _trunk(xs, seq_lens, TM, TME, TT, ln_g, ln_b, a_w_in, a_vn_g, a_vn_b, a_w_s, a_b_s, a_w_out,
                      b_w_in, b_conv_w, b_conv_b, b_w_gates, b_b_gates, b_lambda, b_w_out,
                      router_w, router_b, moe_w_gu, moe_w_down)
    return y_p.reshape(x_prompt.shape), y_s.reshape(x_sample.shape)
```

```python
import functools

import jax
import jax.numpy as jnp
from jax import lax
from jax.experimental import pallas as pl
from jax.experimental.pallas import tpu as pltpu

F32 = jnp.float32
BF16 = jnp.bfloat16
I32 = jnp.int32

D_MODEL = 2048
DEPTH = 4
CHUNK = 128
A_HALF = D_MODEL
A_GROUPS = 16
RNN_WIDTH = D_MODEL
RNN_BLOCKS = 16
RNN_BDIM = RNN_WIDTH // RNN_BLOCKS
LRU_C = 8.0
N_EXPERTS = 16
N_GROUPS = 4
EXPERTS_PER_GROUP = N_EXPERTS // N_GROUPS
D_EXPERT = D_MODEL // 2
DEEPNORM_ALPHA = (2.0 * DEPTH) ** 0.25
LN_EPS = 1e-5

LANES = 128
SUBLANES = 8
VMEM_LIMIT = 56 * 1024 * 1024

TM = 512
TME = 512
TT = 256


def _cparams(sem):
    return pltpu.CompilerParams(dimension_semantics=sem, vmem_limit_bytes=VMEM_LIMIT)


def _resident(shape):
    nd = len(shape)
    return pl.BlockSpec(shape, lambda *_: (0,) * nd, pipeline_mode=pl.Buffered(1))


def _dot(a, b):
    return jnp.dot(a, b, preferred_element_type=F32)


def _ln(x, g, b):
    mu = jnp.mean(x, axis=-1, keepdims=True)
    xc = x - mu
    var = jnp.mean(xc * xc, axis=-1, keepdims=True)
    return xc * lax.rsqrt(var + LN_EPS) * g + b


def _tok_specs(pieces, tm):
    if len(pieces) == 1:
        return [pl.BlockSpec((tm, D_MODEL), lambda i: (i, 0))]
    n_first = pieces[0].shape[0] // tm
    return [pl.BlockSpec((tm, D_MODEL), lambda i: (jnp.minimum(i, n_first - 1), 0)),
            pl.BlockSpec((tm, D_MODEL), lambda i: (jnp.maximum(i - n_first, 0), 0))]


def _tok_tile(refs, n_first):
    if len(refs) == 1:
        return refs[0][...]
    return jnp.where(pl.program_id(0) < n_first, refs[0][...], refs[1][...])


def _a1_body(*refs, tm, n_x, n_first):
    w_ref, vg_ref, vb_ref, ws_ref, bs_ref, y_ref, s_scr = refs[n_x:]
    xb = _tok_tile(refs[:n_x], n_first).astype(BF16)
    v = jax.nn.gelu(_dot(xb, w_ref[:, A_HALF:]))
    v = _ln(v, vg_ref[...], vb_ref[...]).astype(BF16)
    for c in range(tm // CHUNK):
        rows = slice(c * CHUNK, (c + 1) * CHUNK)
        for g in range(A_GROUPS):
            cols = slice(g * LANES, (g + 1) * LANES)
            s_scr[rows, cols] = _dot(ws_ref[g], v[rows, cols]) + bs_ref[:, cols]
    u = jax.nn.gelu(_dot(xb, w_ref[:, :A_HALF]))
    y_ref[...] = (u * s_scr[...]).astype(BF16)


def _a1(xs, w_in, vn_g, vn_b, w_s, bs_full, tm):
    t = sum(x.shape[0] for x in xs)
    return pl.pallas_call(
        functools.partial(_a1_body, tm=tm, n_x=len(xs), n_first=xs[0].shape[0] // tm),
        grid=(t // tm,),
        in_specs=_tok_specs(xs, tm) + [
            _resident((D_MODEL, 2 * A_HALF)),
            _resident((1, A_HALF)),
            _resident((1, A_HALF)),
            _resident((A_GROUPS, CHUNK, CHUNK)),
            _resident((CHUNK, A_HALF)),
        ],
        out_specs=pl.BlockSpec((tm, A_HALF), lambda i: (i, 0)),
        out_shape=jax.ShapeDtypeStruct((t, A_HALF), BF16),
        scratch_shapes=[pltpu.VMEM((tm, A_HALF), F32)],
        compiler_params=_cparams(("arbitrary",)),
        name="a1_gmlp_front",
    )(*xs, w_in, vn_g, vn_b, w_s, bs_full)


def _half_rows(tm):
    return [slice(0, tm // 2), slice(tm // 2, tm)]


def _op_a_body(*refs, tm, n_x, n_first):
    y_ref, w_ref = refs[:2]
    g_ref, b_ref, o_ref = refs[2 + n_x:]
    x = _tok_tile(refs[2:2 + n_x], n_first)
    for rows in _half_rows(tm):
        m = _dot(y_ref[rows, :], w_ref[...])
        o_ref[rows, :] = _ln(DEEPNORM_ALPHA * x[rows, :] + m, g_ref[...], b_ref[...])


def _op_b_body(*refs, tm, n_x, n_first):
    hf_ref, hb_ref, gg_ref, w_ref = refs[:4]
    g_ref, b_ref, o_ref = refs[4 + n_x:]
    x = _tok_tile(refs[4:4 + n_x], n_first)
    for rows in _half_rows(tm):
        y = ((hf_ref[rows, :] + hb_ref[rows, :]) * gg_ref[rows, :].astype(F32)).astype(BF16)
        m = _dot(y, w_ref[...])
        o_ref[rows, :] = _ln(DEEPNORM_ALPHA * x[rows, :] + m, g_ref[...], b_ref[...])


def _out_proj(body, acts, w_out, xs, g, b, tm, name):
    t = sum(x.shape[0] for x in xs)
    tile = pl.BlockSpec((tm, D_MODEL), lambda i: (i, 0))
    return pl.pallas_call(
        functools.partial(body, tm=tm, n_x=len(xs), n_first=xs[0].shape[0] // tm),
        grid=(t // tm,),
        in_specs=([tile] * len(acts) + [_resident(w_out.shape)] + _tok_specs(xs, tm)
                  + [_resident((1, D_MODEL)), _resident((1, D_MODEL))]),
        out_specs=tile,
        out_shape=jax.ShapeDtypeStruct((t, D_MODEL), F32),
        compiler_params=_cparams(("arbitrary",)),
        name=name,
    )(*acts, w_out, *xs, g, b)


def _b1_body(x_ref, w_ref, gg_ref, xr_ref):
    xb = x_ref[...].astype(BF16)
    gg_ref[...] = jax.nn.gelu(_dot(xb, w_ref[:, :RNN_WIDTH])).astype(BF16)
    xr_ref[...] = _dot(xb, w_ref[:, RNN_WIDTH:])


def _b1(x, w_in, tm):
    t = x.shape[0]
    tile = pl.BlockSpec((tm, D_MODEL), lambda i: (i, 0))
    return pl.pallas_call(
        _b1_body,
        grid=(t // tm,),
        in_specs=[tile, _resident((D_MODEL, 2 * RNN_WIDTH))],
        out_specs=[tile, tile],
        out_shape=[jax.ShapeDtypeStruct((t, RNN_WIDTH), BF16), jax.ShapeDtypeStruct((t, RNN_WIDTH), F32)],
        compiler_params=_cparams(("arbitrary",)),
        name="b1_rglru_front",
    )(x, w_in)


def _scan_rows(a_s, b_s, o_ref, carry_scr, *, tt, reverse):
    n_grp = tt // SUBLANES
    width = 512
    row_id = lax.broadcasted_iota(I32, (SUBLANES, width), 0)

    def step(k, carry):
        g = (n_grp - 1 - k) if reverse else k
        r0 = pl.multiple_of(g * SUBLANES, SUBLANES)
        new = []
        for q in range(RNN_WIDTH // width):
            cols = slice(q * width, (q + 1) * width)
            a = a_s[pl.ds(r0, SUBLANES), cols]
            b = b_s[pl.ds(r0, SUBLANES), cols]
            for d in (1, 2, 4):
                if reverse:
                    keep = row_id < SUBLANES - d
                    sh = SUBLANES - d
                else:
                    keep = row_id >= d
                    sh = d
                a_sh = jnp.where(keep, pltpu.roll(a, sh, axis=0), 1.0)
                b_sh = jnp.where(keep, pltpu.roll(b, sh, axis=0), 0.0)
                b = a * b_sh + b
                a = a * a_sh
            h = a * carry[q] + b
            o_ref[pl.ds(r0, SUBLANES), cols] = h
            edge = h[0:1, :] if reverse else h[SUBLANES - 1:SUBLANES, :]
            new.append(jnp.broadcast_to(edge, (SUBLANES, width)))
        return tuple(new)

    init = tuple(carry_scr[:, q * width:(q + 1) * width] for q in range(RNN_WIDTH // width))
    fin = lax.fori_loop(0, n_grp, step, init)
    for q in range(RNN_WIDTH // width):
        carry_scr[:, q * width:(q + 1) * width] = fin[q]


def _b2_direction(d, x_ref, prev_ref, next_ref, at_start, at_end, reset, cw_ref, cb_ref, wg_ref, bg_ref, sp_ref,
                  o_ref, carry_scr, xc_s, a_s, b_s, *, tt):
    x = x_ref[...]
    prev = jnp.where(at_start, 0.0, prev_ref[...])
    nxt = jnp.where(at_end, 0.0, next_ref[...])
    rid = lax.broadcasted_iota(I32, (SUBLANES, 1), 0)
    last = tt - SUBLANES
    xm1 = pltpu.roll(x, 1, axis=0)
    xm1 = jnp.concatenate([jnp.where(rid == 0, prev[SUBLANES - 1:SUBLANES, :], xm1[0:SUBLANES, :]), xm1[SUBLANES:, :]], axis=0)
    xp1 = pltpu.roll(x, tt - 1, axis=0)
    xp1 = jnp.concatenate([xp1[:last, :], jnp.where(rid == SUBLANES - 1, nxt[0:1, :], xp1[last:, :])], axis=0)
    xp2 = pltpu.roll(x, tt - 2, axis=0)
    tail = jnp.where(rid == SUBLANES - 2, nxt[0:1, :], jnp.where(rid == SUBLANES - 1, nxt[1:2, :], xp2[last:, :]))
    xp2 = jnp.concatenate([xp2[:last, :], tail], axis=0)
    xc_s[...] = (cb_ref[...] + cw_ref[0:1, :] * xm1 + cw_ref[1:2, :] * x
                 + cw_ref[2:3, :] * xp1 + cw_ref[3:4, :] * xp2)
    for h in range(RNN_BLOCKS):
        cols = slice(h * RNN_BDIM, (h + 1) * RNN_BDIM)
        xh = xc_s[:, cols]
        g2 = _dot(xh.astype(BF16), wg_ref[d, h])
        t_r = jnp.tanh(g2[:, :RNN_BDIM] + bg_ref[2 * d:2 * d + 1, cols])
        t_i = jnp.tanh(g2[:, RNN_BDIM:] + bg_ref[2 * d + 1:2 * d + 2, cols])
        c2 = sp_ref[d:d + 1, cols]
        log_a = c2 * t_r + c2
        a = jnp.exp(log_a)
        mult = jnp.sqrt(-jnp.tanh(log_a) * (a * a + 1.0))
        a_s[:, cols] = a
        b_s[:, cols] = mult * ((0.5 * t_i + 0.5) * xh)

    @pl.when(reset)
    def _():
        carry_scr[...] = jnp.zeros_like(carry_scr)

    _scan_rows(a_s, b_s, o_ref, carry_scr, tt=tt, reverse=(d == 1))


def _b2_body(xf_ref, pf_ref, nf_ref, xb_ref, pb_ref, nb_ref, cw_ref, cb_ref, wg_ref, bg_ref, lam_ref,
             hf_ref, hb_ref, cf_scr, cbk_scr, sp_scr, xc_s, a_s, b_s, *, tt, n_t, starts, ends):
    i = pl.program_id(0)
    j = n_t - 1 - i

    def hits(idx, marks):
        r = idx == marks[0]
        for m in marks[1:]:
            r = jnp.logical_or(r, idx == m)
        return r

    sp_scr[...] = (-0.5 * LRU_C) * jax.nn.softplus(-lam_ref[...])
    common = dict(cw_ref=cw_ref, cb_ref=cb_ref, wg_ref=wg_ref, bg_ref=bg_ref, sp_ref=sp_scr,
                  xc_s=xc_s, a_s=a_s, b_s=b_s, tt=tt)
    f_start, f_end = hits(i, starts), hits(i, ends)
    _b2_direction(0, xf_ref, pf_ref, nf_ref, f_start, f_end, f_start, o_ref=hf_ref, carry_scr=cf_scr, **common)
    b_start, b_end = hits(j, starts), hits(j, ends)
    _b2_direction(1, xb_ref, pb_ref, nb_ref, b_start, b_end, b_end, o_ref=hb_ref, carry_scr=cbk_scr, **common)


def _b2(xr, conv_w, conv_b, wg, b_gates, lam, seq_lens, tt):
    t = xr.shape[0]
    n_t = t // tt
    hp = tt // SUBLANES
    n_h = t // SUBLANES
    bounds = [0]
    for s in seq_lens:
        bounds.append(bounds[-1] + s)
    starts = tuple(b // tt for b in bounds[:-1])
    ends = tuple(b // tt - 1 for b in bounds[1:])
    w = RNN_WIDTH
    main_f = pl.BlockSpec((tt, w), lambda i: (i, 0))
    prev_f = pl.BlockSpec((SUBLANES, w), lambda i: (jnp.maximum(i * hp - 1, 0), 0))
    next_f = pl.BlockSpec((SUBLANES, w), lambda i: (jnp.minimum((i + 1) * hp, n_h - 1), 0))
    main_b = pl.BlockSpec((tt, w), lambda i: (n_t - 1 - i, 0))
    prev_b = pl.BlockSpec((SUBLANES, w), lambda i: (jnp.maximum((n_t - 1 - i) * hp - 1, 0), 0))
    next_b = pl.BlockSpec((SUBLANES, w), lambda i: (jnp.minimum((n_t - i) * hp, n_h - 1), 0))
    return pl.pallas_call(
        functools.partial(_b2_body, tt=tt, n_t=n_t, starts=starts, ends=ends),
        grid=(n_t,),
        in_specs=[main_f, prev_f, next_f, main_b, prev_b, next_b,
                  _resident((4, w)), _resident((1, w)), _resident(wg.shape), _resident((4, w)), _resident((2, w))],
        out_specs=[main_f, main_b],
        out_shape=[jax.ShapeDtypeStruct((t, w), F32), jax.ShapeDtypeStruct((t, w), F32)],
        scratch_shapes=[pltpu.VMEM((SUBLANES, w), F32), pltpu.VMEM((SUBLANES, w), F32), pltpu.VMEM((2, w), F32),
                        pltpu.VMEM((tt, w), F32), pltpu.VMEM((tt, w), F32), pltpu.VMEM((tt, w), F32)],
        compiler_params=_cparams(("arbitrary",)),
        name="b2_conv_gates_scan",
    )(xr, xr, xr, xr, xr, xr, conv_w, conv_b, wg, b_gates, lam)


def _top2_of4(a):
    m1 = jnp.maximum(jnp.maximum(a[0], a[1]), jnp.maximum(a[2], a[3]))
    i1 = jnp.where(a[0] == m1, 0, jnp.where(a[1] == m1, 1, jnp.where(a[2] == m1, 2, 3)))
    b = [jnp.where(i1 == k, -1.0, a[k]) for k in range(4)]
    m2 = jnp.maximum(jnp.maximum(b[0], b[1]), jnp.maximum(b[2], b[3]))
    i2 = jnp.where(b[0] == m2, 0, jnp.where(b[1] == m2, 1, jnp.where(b[2] == m2, 2, 3)))
    return m1, i1, m2, i2


def _router_body(x_ref, wh_ref, wl_ref, rb_ref, tri_ref, ert_ref, wb0_ref, wb1_ref, cnt_ref, base_scr, *, tm):
    @pl.when(pl.program_id(0) == 0)
    def _():
        base_scr[...] = jnp.zeros_like(base_scr)

    x = x_ref[...]
    xh = x.astype(BF16)
    xl = (x - xh.astype(F32)).astype(BF16)
    wh = wh_ref[...]
    logits = _dot(xh, wh) + _dot(xl, wh) + _dot(xh, wl_ref[...]) + rb_ref[...]
    lt = jnp.transpose(logits)
    l = [lt[e:e + 1, :] for e in range(N_EXPERTS)]
    mx = l[0]
    for e in range(1, N_EXPERTS):
        mx = jnp.maximum(mx, l[e])
    ex = [jnp.exp(v - mx) for v in l]
    den = ex[0]
    for e in range(1, N_EXPERTS):
        den = den + ex[e]
    p = [v / den for v in ex]

    tops = [_top2_of4(p[g * EXPERTS_PER_GROUP:(g + 1) * EXPERTS_PER_GROUP]) for g in range(N_GROUPS)]
    score = [t[0] + t[2] for t in tops]
    best = jnp.maximum(jnp.maximum(score[0], score[1]), jnp.maximum(score[2], score[3]))
    gsel = jnp.where(score[0] == best, 0, jnp.where(score[1] == best, 1, jnp.where(score[2] == best, 2, 3)))

    def pick(field):
        return jnp.where(gsel == 0, tops[0][field],
                         jnp.where(gsel == 1, tops[1][field], jnp.where(gsel == 2, tops[2][field], tops[3][field])))

    v1, i1, v2, i2 = pick(0), pick(1), pick(2), pick(3)
    e1 = gsel * EXPERTS_PER_GROUP + i1
    e2 = gsel * EXPERTS_PER_GROUP + i2
    vs = v1 + v2
    w1 = v1 / vs
    w2 = v2 / vs

    eid = lax.broadcasted_iota(I32, (N_EXPERTS, tm), 0)
    hit1 = eid == e1
    hit2 = eid == e2
    oh = jnp.where(hit1, 1.0, 0.0) + jnp.where(hit2, 1.0, 0.0)
    before = _dot(oh.astype(BF16), tri_ref[...])
    tot = before + jnp.concatenate([base_scr[...]] * (tm // LANES), axis=1)
    r1 = jnp.sum(jnp.where(hit1, tot, 0.0), axis=0, keepdims=True)
    r2 = jnp.sum(jnp.where(hit2, tot, 0.0), axis=0, keepdims=True)
    base_scr[...] = base_scr[...] + jnp.sum(oh, axis=1, keepdims=True)
    cnt_ref[...] = base_scr[...]

    zero = jnp.zeros((1, tm), I32)
    ert_ref[...] = jnp.concatenate([e1, e2, r1.astype(I32), r2.astype(I32), zero, zero, zero, zero], axis=0)
    wb0_ref[...] = jnp.transpose(jnp.broadcast_to(w1, (LANES, tm)))
    wb1_ref[...] = jnp.transpose(jnp.broadcast_to(w2, (LANES, tm)))


def _router(x, rwh, rwl, rb, tri, tm):
    t = x.shape[0]
    return pl.pallas_call(
        functools.partial(_router_body, tm=tm),
        grid=(t // tm,),
        in_specs=[pl.BlockSpec((tm, D_MODEL), lambda i: (i, 0)),
                  _resident((D_MODEL, LANES)), _resident((D_MODEL, LANES)), _resident((1, LANES)), _resident((tm, tm))],
        out_specs=[pl.BlockSpec((SUBLANES, tm), lambda i: (0, i)),
                   pl.BlockSpec((tm, LANES), lambda i: (i, 0)),
                   pl.BlockSpec((tm, LANES), lambda i: (i, 0)),
                   pl.BlockSpec((N_EXPERTS, LANES), lambda i: (0, 0))],
        out_shape=[jax.ShapeDtypeStruct((SUBLANES, t), I32),
                   jax.ShapeDtypeStruct((t, LANES), F32),
                   jax.ShapeDtypeStruct((t, LANES), F32),
                   jax.ShapeDtypeStruct((N_EXPERTS, LANES), F32)],
        scratch_shapes=[pltpu.VMEM((N_EXPERTS, LANES), F32)],
        compiler_params=_cparams(("arbitrary",)),
        name="moe_router",
    )(x, rwh, rwl, rb, tri)


def _row_copy(src, s, dst, d, sem):
    return pltpu.make_async_copy(src.at[pl.ds(s, 1), :], dst.at[pl.ds(d, 1), :], sem)


def _rows_wait(hbm, n, sem):
    pltpu.make_async_copy(hbm.at[pl.ds(0, n), :], hbm.at[pl.ds(0, n), :], sem).wait()


def _dispatch_body(pad_lo_ref, pad_hi_ref, p0_ref, p1_ref, x_ref, xs_out, zrow, sem, zsem, *, tm):
    @pl.when(pl.program_id(0) == 0)
    def _():
        zrow[...] = jnp.zeros_like(zrow)
        for e in range(N_EXPERTS):
            def put(r, c):
                _row_copy(zrow, 0, xs_out, r, zsem).start()
                return c

            lax.fori_loop(pad_lo_ref[e], pad_hi_ref[e], put, 0)
        for e in range(N_EXPERTS):
            def got(r, c):
                _row_copy(zrow, 0, xs_out, 0, zsem).wait()
                return c

            lax.fori_loop(pad_lo_ref[e], pad_hi_ref[e], got, 0)

    def issue(g, c):
        t0 = g * SUBLANES
        for s in range(SUBLANES):
            for k, p_ref in enumerate((p0_ref, p1_ref)):
                _row_copy(x_ref.at[g], s, xs_out, p_ref[t0 + s], sem).start(priority=k)
        return c

    lax.fori_loop(0, tm // SUBLANES, issue, 0)
    _rows_wait(xs_out, 2 * tm, sem)


def _dispatch(pad_lo, pad_hi, pos0, pos1, x, n_rows, tm):
    t = x.shape[0]
    idx = pl.BlockSpec((tm,), lambda i, lo, hi: (i,), memory_space=pltpu.SMEM)
    return pl.pallas_call(
        functools.partial(_dispatch_body, tm=tm),
        grid_spec=pltpu.PrefetchScalarGridSpec(
            num_scalar_prefetch=2,
            grid=(t // tm,),
            in_specs=[idx, idx, pl.BlockSpec((tm // SUBLANES, SUBLANES, D_MODEL), lambda i, lo, hi: (i, 0, 0))],
            out_specs=pl.BlockSpec(memory_space=pl.ANY),
            scratch_shapes=[pltpu.VMEM((SUBLANES, D_MODEL), F32),
                            pltpu.SemaphoreType.DMA(()), pltpu.SemaphoreType.DMA(())]),
        out_shape=jax.ShapeDtypeStruct((n_rows, D_MODEL), F32),
        compiler_params=pltpu.CompilerParams(dimension_semantics=("arbitrary",), vmem_limit_bytes=VMEM_LIMIT,
                                             has_side_effects=True, disable_bounds_checks=True),
        name="moe_dispatch",
    )(pad_lo, pad_hi, pos0, pos1, x.reshape(t // SUBLANES, SUBLANES, D_MODEL))


def _ffn_body(te_ref, nu_ref, xs_ref, wgu_ref, wdn_ref, ys_ref):
    used = pl.program_id(0) < nu_ref[0]

    @pl.when(used)
    def _():
        h = _dot(xs_ref[...].astype(BF16), wgu_ref[0, 0])
        hh = (jax.nn.silu(h[:, :D_EXPERT]) * h[:, D_EXPERT:]).astype(BF16)
        ys_ref[...] = _dot(hh, wdn_ref[0, 0])

    @pl.when(jnp.logical_not(used))
    def _():
        ys_ref[...] = jnp.zeros_like(ys_ref)


def _ffn(tile_expert, n_used, xs, w_gu, w_down, layer, tm):
    p = xs.shape[0]
    row_tile = pl.BlockSpec((tm, D_MODEL), lambda i, te, nu: (i, 0))
    return pl.pallas_call(
        _ffn_body,
        grid_spec=pltpu.PrefetchScalarGridSpec(
            num_scalar_prefetch=2,
            grid=(p // tm,),
            in_specs=[row_tile,
                      pl.BlockSpec((1, 1, D_MODEL, 2 * D_EXPERT), lambda i, te, nu: (layer, te[i], 0, 0)),
                      pl.BlockSpec((1, 1, D_EXPERT, D_MODEL), lambda i, te, nu: (layer, te[i], 0, 0))],
            out_specs=row_tile),
        out_shape=jax.ShapeDtypeStruct((p, D_MODEL), F32),
        compiler_params=_cparams(("arbitrary",)),
        name="moe_expert_ffn",
    )(tile_expert, n_used, xs, w_gu, w_down)


def _combine_body(p0_ref, p1_ref, p0n_ref, p1n_ref, ys_hbm, wb0_ref, wb1_ref, x_ref, g_ref, b_ref, *rest,
                  tm, n_t, n_first):
    o_refs = rest[:-6]
    a0, a1, b0, b1, sem_a, sem_b = rest[-6:]
    i = pl.program_id(0)

    def gather(p_refs, bufs, sem):
        def issue(g, c):
            t0 = g * SUBLANES
            for s in range(SUBLANES):
                for k in range(2):
                    _row_copy(ys_hbm, p_refs[k][t0 + s], bufs[k].at[g], s, sem).start(priority=k)
            return c

        lax.fori_loop(0, tm // SUBLANES, issue, 0)

    def reduce(bufs, sem):
        _rows_wait(ys_hbm, 2 * tm, sem)
        reps = D_MODEL // LANES
        w0 = jnp.concatenate([wb0_ref[...]] * reps, axis=1)
        w1 = jnp.concatenate([wb1_ref[...]] * reps, axis=1)
        y0 = bufs[0][...].reshape(tm, D_MODEL)
        y1 = bufs[1][...].reshape(tm, D_MODEL)
        out = _ln(DEEPNORM_ALPHA * x_ref[...] + (w0 * y0 + w1 * y1), g_ref[...], b_ref[...])
        if len(o_refs) == 1:
            o_refs[0][...] = out
        else:
            @pl.when(i < n_first)
            def _():
                o_refs[0][...] = out

            @pl.when(i >= n_first)
            def _():
                o_refs[1][...] = out

    slots = (((a0, a1), sem_a), ((b0, b1), sem_b))

    @pl.when(i == 0)
    def _():
        gather((p0_ref, p1_ref), *slots[0])

    for s in range(2):
        @pl.when(i % 2 == s)
        def _():
            @pl.when(i + 1 < n_t)
            def _():
                gather((p0n_ref, p1n_ref), *slots[1 - s])

            reduce(*slots[s])


def _combine(pos0, pos1, ys, wb0, wb1, x, g, b, tm, split_rows=None):
    t = x.shape[0]
    n_t = t // tm
    out_rows = (t,) if split_rows is None else (split_rows, t - split_rows)
    outs = [jax.ShapeDtypeStruct((r, D_MODEL), F32) for r in out_rows]
    tile = pl.BlockSpec((tm, D_MODEL), lambda i: (i, 0))
    wtile = pl.BlockSpec((tm, LANES), lambda i: (i, 0))
    vec = pl.BlockSpec((1, D_MODEL), lambda i: (0, 0))
    idx = pl.BlockSpec((tm,), lambda i: (i,), memory_space=pltpu.SMEM)
    idx_next = pl.BlockSpec((tm,), lambda i: (jnp.minimum(i + 1, n_t - 1),), memory_space=pltpu.SMEM)
    rows = pltpu.VMEM((tm // SUBLANES, SUBLANES, D_MODEL), F32)
    res = pl.pallas_call(
        functools.partial(_combine_body, tm=tm, n_t=n_t, n_first=out_rows[0] // tm),
        grid=(n_t,),
        in_specs=[idx, idx, idx_next, idx_next, pl.BlockSpec(memory_space=pl.ANY), wtile, wtile, tile, vec, vec],
        out_specs=_tok_specs(outs, tm),
        scratch_shapes=[rows, rows, rows, rows, pltpu.SemaphoreType.DMA(()), pltpu.SemaphoreType.DMA(())],
        out_shape=outs,
        compiler_params=pltpu.CompilerParams(dimension_semantics=("arbitrary",), vmem_limit_bytes=VMEM_LIMIT,
                                             disable_bounds_checks=True),
        name="moe_combine",
    )(pos0, pos1, pos0, pos1, ys, wb0, wb1, x, g, b)
    return res[0] if split_rows is None else tuple(res)


def _moe_layer(x, rwh, rwl, rb, tri, w_gu, w_down, layer, g, b, tm, tme, split_rows=None):
    t = x.shape[0]
    n_tiles = (2 * t) // tme + N_EXPERTS
    ert, wb0, wb1, cnt = _router(x, rwh, rwl, rb, tri, tm)
    counts = cnt[:, 0].astype(I32)
    padded = ((counts + tme - 1) // tme) * tme
    ends = jnp.cumsum(padded).astype(I32)
    off = ends - padded
    n_used = ends[-1:] // tme
    tile_start = jnp.arange(n_tiles, dtype=I32) * tme
    tile_expert = jnp.minimum(jnp.sum(tile_start[:, None] >= ends[None, :], axis=1), N_EXPERTS - 1).astype(I32)
    is_e = ert[0:2, :, None] == jnp.arange(N_EXPERTS, dtype=I32)
    pos = jnp.sum(jnp.where(is_e, off, 0), axis=-1) + ert[2:4]
    pad_hi = ends.at[N_EXPERTS - 1].set(n_tiles * tme)
    xs = _dispatch(off + counts, pad_hi, pos[0], pos[1], x, n_tiles * tme, tm)
    ys = _ffn(tile_expert, n_used, xs, w_gu, w_down, layer, tme)
    return _combine(pos[0], pos[1], ys, wb0, wb1, x, g, b, tm, split_rows)


def _trunk(xs, seq_lens, tm, tme, tt, ln_g, ln_b, a_w_in, a_vn_g, a_vn_b, a_w_s, a_b_s, a_w_out,
           b_w_in, b_conv_w, b_conv_b, b_w_gates, b_b_gates, b_lambda, b_w_out,
           router_w, router_b, moe_w_gu, moe_w_down):
    row = lambda v: v.reshape(1, -1).astype(F32)
    rw = jnp.pad(router_w.astype(F32), ((0, 0), (0, LANES - N_EXPERTS)))
    rwh = rw.astype(BF16)
    rwl = (rw - rwh.astype(F32)).astype(BF16)
    rb = jnp.pad(router_b.astype(F32), (0, LANES - N_EXPERTS)).reshape(1, LANES)
    tri = (jnp.arange(tm)[:, None] < jnp.arange(tm)[None, :]).astype(BF16)
    w_gu, w_down = moe_w_gu.astype(BF16), moe_w_down.astype(BF16)
    ia = ib = 0
    x = None
    for layer in range(DEPTH):
        xin = xs if x is None else (x,)
        last = layer == DEPTH - 1
        g0, b0 = row(ln_g[layer, 0]), row(ln_b[layer, 0])
        if layer % 2 == 0:
            bs_full = jnp.repeat(a_b_s[ia].T.astype(F32), CHUNK, axis=1)
            y = _a1(xin, a_w_in[ia].astype(BF16), row(a_vn_g[ia]), row(a_vn_b[ia]), a_w_s[ia].astype(BF16), bs_full, tm)
            x = _out_proj(_op_a_body, [y], a_w_out[ia].astype(BF16), xin, g0, b0, tm, "op_a")
            ia += 1
        else:
            gg, xr = _b1(x, b_w_in[ib].astype(BF16), tm)
            wgt = (0.5 * b_w_gates[ib]).astype(BF16)
            wg = jnp.stack([jnp.concatenate([wgt[0], wgt[1]], axis=-1),
                            jnp.concatenate([wgt[2], wgt[3]], axis=-1)])
            hf, hb = _b2(xr, b_conv_w[ib].astype(F32), row(b_conv_b[ib]), wg, 0.5 * b_b_gates[ib].astype(F32),
                         b_lambda[ib].astype(F32), seq_lens, tt)
            x = _out_proj(_op_b_body, [hf, hb, gg], b_w_out[ib].astype(BF16), (x,), g0, b0, tm, "op_b")
            ib += 1
        x = _moe_layer(x, rwh, rwl, rb, tri, w_gu, w_down, layer,
                       row(ln_g[layer, 1]), row(ln_b[layer, 1]), tm, tme,
                       split_rows=xs[0].shape[0] if last else None)
    return x


def kernel(x_prompt, x_sample, ln_g, ln_b, a_w_in, a_vn_g, a_vn_b, a_w_s, a_b_s, a_w_out, b_w_in, b_conv_w, b_conv_b,
           b_w_gates, b_b_gates, b_lambda, b_w_out, router_w, router_b, moe_w_gu, moe_w_down):
    d = x_prompt.shape[-1]
    seq_lens = (x_prompt.shape[1],) * x_prompt.shape[0] + (x_sample.shape[1],) * x_sample.shape[0]
    xs = (x_prompt.reshape(-1, d), x_sample.reshape(-1, d))
    y_p, y_s = _trunk(xs, seq_lens, TM, TME, TT, ln_g, ln_b, a_w_in, a_vn_g, a_vn_b, a_w_s, a_b_s, a_w_out,
                      b_w_in, b_conv_w, b_conv_b, b_w_gates, b_b_gates, b_lambda, b_w_out,
                      router_w, router_b, moe_w_gu, moe_w_down)
    return y_p.reshape(x_prompt.shape), y_s.reshape(x_sample.shape)
```

```python
import functools

import jax
import jax.numpy as jnp
from jax import lax
from jax.experimental import pallas as pl
from jax.experimental.pallas import tpu as pltpu

F32 = jnp.float32
BF16 = jnp.bfloat16
I32 = jnp.int32

D_MODEL = 2048
DEPTH = 4
CHUNK = 128
A_HALF = D_MODEL
A_GROUPS = 16
RNN_WIDTH = D_MODEL
RNN_BLOCKS = 16
RNN_BDIM = RNN_WIDTH // RNN_BLOCKS
LRU_C = 8.0
N_EXPERTS = 16
N_GROUPS = 4
EXPERTS_PER_GROUP = N_EXPERTS // N_GROUPS
D_EXPERT = D_MODEL // 2
DEEPNORM_ALPHA = (2.0 * DEPTH) ** 0.25
LN_EPS = 1e-5

LANES = 128
SUBLANES = 8
VMEM_LIMIT = 56 * 1024 * 1024

TM = 512
TME = 512
TT = 256


def _cparams(sem):
    return pltpu.CompilerParams(dimension_semantics=sem, vmem_limit_bytes=VMEM_LIMIT)


def _resident(shape):
    nd = len(shape)
    return pl.BlockSpec(shape, lambda *_: (0,) * nd, pipeline_mode=pl.Buffered(1))


def _dot(a, b):
    return jnp.dot(a, b, preferred_element_type=F32)


def _ln(x, g, b):
    mu = jnp.mean(x, axis=-1, keepdims=True)
    xc = x - mu
    var = jnp.mean(xc * xc, axis=-1, keepdims=True)
    return xc * lax.rsqrt(var + LN_EPS) * g + b


def _tok_specs(pieces, tm):
    if len(pieces) == 1:
        return [pl.BlockSpec((tm, D_MODEL), lambda i: (i, 0))]
    n_first = pieces[0].shape[0] // tm
    return [pl.BlockSpec((tm, D_MODEL), lambda i: (jnp.minimum(i, n_first - 1), 0)),
            pl.BlockSpec((tm, D_MODEL), lambda i: (jnp.maximum(i - n_first, 0), 0))]


def _tok_tile(refs, n_first):
    if len(refs) == 1:
        return refs[0][...]
    return jnp.where(pl.program_id(0) < n_first, refs[0][...], refs[1][...])


def _a1_body(*refs, tm, n_x, n_first):
    w_ref, vg_ref, vb_ref, ws_ref, bs_ref, y_ref, s_scr = refs[n_x:]
    xb = _tok_tile(refs[:n_x], n_first).astype(BF16)
    v = jax.nn.gelu(_dot(xb, w_ref[:, A_HALF:]))
    v = _ln(v, vg_ref[...], vb_ref[...]).astype(BF16)
    for c in range(tm // CHUNK):
        rows = slice(c * CHUNK, (c + 1) * CHUNK)
        for g in range(A_GROUPS):
            cols = slice(g * LANES, (g + 1) * LANES)
            s_scr[rows, cols] = _dot(ws_ref[g], v[rows, cols]) + bs_ref[:, cols]
    u = jax.nn.gelu(_dot(xb, w_ref[:, :A_HALF]))
    y_ref[...] = (u * s_scr[...]).astype(BF16)


def _a1(xs, w_in, vn_g, vn_b, w_s, bs_full, tm):
    t = sum(x.shape[0] for x in xs)
    return pl.pallas_call(
        functools.partial(_a1_body, tm=tm, n_x=len(xs), n_first=xs[0].shape[0] // tm),
        grid=(t // tm,),
        in_specs=_tok_specs(xs, tm) + [
            _resident((D_MODEL, 2 * A_HALF)),
            _resident((1, A_HALF)),
            _resident((1, A_HALF)),
            _resident((A_GROUPS, CHUNK, CHUNK)),
            _resident((CHUNK, A_HALF)),
        ],
        out_specs=pl.BlockSpec((tm, A_HALF), lambda i: (i, 0)),
        out_shape=jax.ShapeDtypeStruct((t, A_HALF), BF16),
        scratch_shapes=[pltpu.VMEM((tm, A_HALF), F32)],
        compiler_params=_cparams(("arbitrary",)),
        name="a1_gmlp_front",
    )(*xs, w_in, vn_g, vn_b, w_s, bs_full)


def _half_rows(tm):
    return [slice(0, tm // 2), slice(tm // 2, tm)]


def _op_a_body(*refs, tm, n_x, n_first):
    y_ref, w_ref = refs[:2]
    g_ref, b_ref, o_ref = refs[2 + n_x:]
    x = _tok_tile(refs[2:2 + n_x], n_first)
    for rows in _half_rows(tm):
        m = _dot(y_ref[rows, :], w_ref[...])
        o_ref[rows, :] = _ln(DEEPNORM_ALPHA * x[rows, :] + m, g_ref[...], b_ref[...])


def _op_b_body(*refs, tm, n_x, n_first):
    hf_ref, hb_ref, gg_ref, w_ref = refs[:4]
    g_ref, b_ref, o_ref = refs[4 + n_x:]
    x = _tok_tile(refs[4:4 + n_x], n_first)
    for rows in _half_rows(tm):
        y = ((hf_ref[rows, :] + hb_ref[rows, :]) * gg_ref[rows, :].astype(F32)).astype(BF16)
        m = _dot(y, w_ref[...])
        o_ref[rows, :] = _ln(DEEPNORM_ALPHA * x[rows, :] + m, g_ref[...], b_ref[...])


def _out_proj(body, acts, w_out, xs, g, b, tm, name):
    t = sum(x.shape[0] for x in xs)
    tile = pl.BlockSpec((tm, D_MODEL), lambda i: (i, 0))
    return pl.pallas_call(
        functools.partial(body, tm=tm, n_x=len(xs), n_first=xs[0].shape[0] // tm),
        grid=(t // tm,),
        in_specs=([tile] * len(acts) + [_resident(w_out.shape)] + _tok_specs(xs, tm)
                  + [_resident((1, D_MODEL)), _resident((1, D_MODEL))]),
        out_specs=tile,
        out_shape=jax.ShapeDtypeStruct((t, D_MODEL), F32),
        compiler_params=_cparams(("arbitrary",)),
        name=name,
    )(*acts, w_out, *xs, g, b)


def _edge_hits(idx, marks):
    r = idx == marks[0]
    for m in marks[1:]:
        r = jnp.logical_or(r, idx == m)
    return r


def _seq_edge_tiles(seq_lens, tile):
    bounds = [0]
    for s in seq_lens:
        bounds.append(bounds[-1] + s)
    return tuple(b // tile for b in bounds[:-1]), tuple(b // tile - 1 for b in bounds[1:])


def _b1_body(x_ref, xp_ref, xn_ref, w_ref, cw_ref, cb_ref, gg_ref, xc_ref, *, tm, starts, ends):
    i = pl.program_id(0)
    prev = jnp.where(_edge_hits(i, starts), 0.0, xp_ref[...])
    nxt = jnp.where(_edge_hits(i, ends), 0.0, xn_ref[...])
    xe = jnp.concatenate([prev, x_ref[...], nxt], axis=0).astype(BF16)
    rows = slice(SUBLANES, SUBLANES + tm)
    n = tm + 2 * SUBLANES
    width = 512
    for q in range(RNN_WIDTH // width):
        cols = slice(q * width, (q + 1) * width)
        gg_ref[:, cols] = jax.nn.gelu(_dot(xe, w_ref[:, q * width:(q + 1) * width])[rows, :]).astype(BF16)
        xr = _dot(xe, w_ref[:, RNN_WIDTH + q * width:RNN_WIDTH + (q + 1) * width])
        xc_ref[:, cols] = (cb_ref[:, cols] + cw_ref[0:1, cols] * pltpu.roll(xr, 1, axis=0)[rows, :]
                           + cw_ref[1:2, cols] * xr[rows, :]
                           + cw_ref[2:3, cols] * pltpu.roll(xr, n - 1, axis=0)[rows, :]
                           + cw_ref[3:4, cols] * pltpu.roll(xr, n - 2, axis=0)[rows, :])


def _b1(x, w_in, conv_w, conv_b, seq_lens, tm):
    t = x.shape[0]
    hp = tm // SUBLANES
    n_h = t // SUBLANES
    starts, ends = _seq_edge_tiles(seq_lens, tm)
    tile = pl.BlockSpec((tm, D_MODEL), lambda i: (i, 0))
    prev = pl.BlockSpec((SUBLANES, D_MODEL), lambda i: (jnp.maximum(i * hp - 1, 0), 0))
    nxt = pl.BlockSpec((SUBLANES, D_MODEL), lambda i: (jnp.minimum((i + 1) * hp, n_h - 1), 0))
    return pl.pallas_call(
        functools.partial(_b1_body, tm=tm, starts=starts, ends=ends),
        grid=(t // tm,),
        in_specs=[tile, prev, nxt, _resident((D_MODEL, 2 * RNN_WIDTH)), _resident((4, RNN_WIDTH)), _resident((1, RNN_WIDTH))],
        out_specs=[tile, tile],
        out_shape=[jax.ShapeDtypeStruct((t, RNN_WIDTH), BF16), jax.ShapeDtypeStruct((t, RNN_WIDTH), F32)],
        compiler_params=_cparams(("arbitrary",)),
        name="b1_rglru_front",
    )(x, x, x, w_in, conv_w, conv_b)


def _scan_rows(a_s, b_s, o_ref, carry_scr, *, tt, reverse):
    n_grp = tt // SUBLANES
    width = 512
    row_id = lax.broadcasted_iota(I32, (SUBLANES, width), 0)

    def step(k, carry):
        g = (n_grp - 1 - k) if reverse else k
        r0 = pl.multiple_of(g * SUBLANES, SUBLANES)
        new = []
        for q in range(RNN_WIDTH // width):
            cols = slice(q * width, (q + 1) * width)
            a = a_s[pl.ds(r0, SUBLANES), cols]
            b = b_s[pl.ds(r0, SUBLANES), cols]
            for d in (1, 2, 4):
                if reverse:
                    keep = row_id < SUBLANES - d
                    sh = SUBLANES - d
                else:
                    keep = row_id >= d
                    sh = d
                a_sh = jnp.where(keep, pltpu.roll(a, sh, axis=0), 1.0)
                b_sh = jnp.where(keep, pltpu.roll(b, sh, axis=0), 0.0)
                b = a * b_sh + b
                a = a * a_sh
            h = a * carry[q] + b
            o_ref[pl.ds(r0, SUBLANES), cols] = h
            edge = h[0:1, :] if reverse else h[SUBLANES - 1:SUBLANES, :]
            new.append(jnp.broadcast_to(edge, (SUBLANES, width)))
        return tuple(new)

    init = tuple(carry_scr[:, q * width:(q + 1) * width] for q in range(RNN_WIDTH // width))
    fin = lax.fori_loop(0, n_grp, step, init)
    for q in range(RNN_WIDTH // width):
        carry_scr[:, q * width:(q + 1) * width] = fin[q]


def _b2_direction(d, x_ref, reset, wg_ref, bg_ref, sp_ref, o_ref, carry_scr, a_s, b_s, *, tt):
    for h in range(RNN_BLOCKS):
        cols = slice(h * RNN_BDIM, (h + 1) * RNN_BDIM)
        xh = x_ref[:, cols]
        g2 = _dot(xh.astype(BF16), wg_ref[d, h])
        t_r = jnp.tanh(g2[:, :RNN_BDIM] + bg_ref[2 * d:2 * d + 1, cols])
        t_i = jnp.tanh(g2[:, RNN_BDIM:] + bg_ref[2 * d + 1:2 * d + 2, cols])
        c2 = sp_ref[d:d + 1, cols]
        log_a = c2 * t_r + c2
        a = jnp.exp(log_a)
        mult = jnp.sqrt(-jnp.tanh(log_a) * (a * a + 1.0))
        a_s[:, cols] = a
        b_s[:, cols] = mult * ((0.5 * t_i + 0.5) * xh)

    @pl.when(reset)
    def _():
        carry_scr[...] = jnp.zeros_like(carry_scr)

    _scan_rows(a_s, b_s, o_ref, carry_scr, tt=tt, reverse=(d == 1))


def _b2_body(xf_ref, xb_ref, wg_ref, bg_ref, lam_ref, hf_ref, hb_ref, cf_scr, cbk_scr, sp_scr, a_s, b_s,
             *, tt, n_t, starts, ends):
    i = pl.program_id(0)
    j = n_t - 1 - i
    sp_scr[...] = (-0.5 * LRU_C) * jax.nn.softplus(-lam_ref[...])
    common = dict(wg_ref=wg_ref, bg_ref=bg_ref, sp_ref=sp_scr, a_s=a_s, b_s=b_s, tt=tt)
    _b2_direction(0, xf_ref, _edge_hits(i, starts), o_ref=hf_ref, carry_scr=cf_scr, **common)
    _b2_direction(1, xb_ref, _edge_hits(j, ends), o_ref=hb_ref, carry_scr=cbk_scr, **common)


def _b2(xc, wg, b_gates, lam, seq_lens, tt):
    t = xc.shape[0]
    n_t = t // tt
    starts, ends = _seq_edge_tiles(seq_lens, tt)
    w = RNN_WIDTH
    main_f = pl.BlockSpec((tt, w), lambda i: (i, 0))
    main_b = pl.BlockSpec((tt, w), lambda i: (n_t - 1 - i, 0))
    return pl.pallas_call(
        functools.partial(_b2_body, tt=tt, n_t=n_t, starts=starts, ends=ends),
        grid=(n_t,),
        in_specs=[main_f, main_b, _resident(wg.shape), _resident((4, w)), _resident((2, w))],
        out_specs=[main_f, main_b],
        out_shape=[jax.ShapeDtypeStruct((t, w), F32), jax.ShapeDtypeStruct((t, w), F32)],
        scratch_shapes=[pltpu.VMEM((SUBLANES, w), F32), pltpu.VMEM((SUBLANES, w), F32), pltpu.VMEM((2, w), F32),
                        pltpu.VMEM((tt, w), F32), pltpu.VMEM((tt, w), F32)],
        compiler_params=_cparams(("arbitrary",)),
        name="b2_gates_scan",
    )(xc, xc, wg, b_gates, lam)


def _top2_of4(a):
    m1 = jnp.maximum(jnp.maximum(a[0], a[1]), jnp.maximum(a[2], a[3]))
    i1 = jnp.where(a[0] == m1, 0, jnp.where(a[1] == m1, 1, jnp.where(a[2] == m1, 2, 3)))
    b = [jnp.where(i1 == k, -1.0, a[k]) for k in range(4)]
    m2 = jnp.maximum(jnp.maximum(b[0], b[1]), jnp.maximum(b[2], b[3]))
    i2 = jnp.where(b[0] == m2, 0, jnp.where(b[1] == m2, 1, jnp.where(b[2] == m2, 2, 3)))
    return m1, i1, m2, i2


def _router_body(x_ref, wh_ref, wl_ref, rb_ref, tri_ref, ert_ref, wb0_ref, wb1_ref, cnt_ref, base_scr, *, tm):
    @pl.when(pl.program_id(0) == 0)
    def _():
        base_scr[...] = jnp.zeros_like(base_scr)

    x = x_ref[...]
    xh = x.astype(BF16)
    xl = (x - xh.astype(F32)).astype(BF16)
    wh = wh_ref[...]
    logits = _dot(xh, wh) + _dot(xl, wh) + _dot(xh, wl_ref[...]) + rb_ref[...]
    lt = jnp.transpose(logits)
    l = [lt[e:e + 1, :] for e in range(N_EXPERTS)]
    mx = l[0]
    for e in range(1, N_EXPERTS):
        mx = jnp.maximum(mx, l[e])
    ex = [jnp.exp(v - mx) for v in l]
    den = ex[0]
    for e in range(1, N_EXPERTS):
        den = den + ex[e]
    p = [v / den for v in ex]

    tops = [_top2_of4(p[g * EXPERTS_PER_GROUP:(g + 1) * EXPERTS_PER_GROUP]) for g in range(N_GROUPS)]
    score = [t[0] + t[2] for t in tops]
    best = jnp.maximum(jnp.maximum(score[0], score[1]), jnp.maximum(score[2], score[3]))
    gsel = jnp.where(score[0] == best, 0, jnp.where(score[1] == best, 1, jnp.where(score[2] == best, 2, 3)))

    def pick(field):
        return jnp.where(gsel == 0, tops[0][field],
                         jnp.where(gsel == 1, tops[1][field], jnp.where(gsel == 2, tops[2][field], tops[3][field])))

    v1, i1, v2, i2 = pick(0), pick(1), pick(2), pick(3)
    e1 = gsel * EXPERTS_PER_GROUP + i1
    e2 = gsel * EXPERTS_PER_GROUP + i2
    vs = v1 + v2
    w1 = v1 / vs
    w2 = v2 / vs

    eid = lax.broadcasted_iota(I32, (N_EXPERTS, tm), 0)
    hit1 = eid == e1
    hit2 = eid == e2
    oh = jnp.where(hit1, 1.0, 0.0) + jnp.where(hit2, 1.0, 0.0)
    before = _dot(oh.astype(BF16), tri_ref[...])
    tot = before + jnp.concatenate([base_scr[...]] * (tm // LANES), axis=1)
    r1 = jnp.sum(jnp.where(hit1, tot, 0.0), axis=0, keepdims=True)
    r2 = jnp.sum(jnp.where(hit2, tot, 0.0), axis=0, keepdims=True)
    base_scr[...] = base_scr[...] + jnp.sum(oh, axis=1, keepdims=True)
    cnt_ref[...] = base_scr[...]

    zero = jnp.zeros((1, tm), I32)
    ert_ref[...] = jnp.concatenate([e1, e2, r1.astype(I32), r2.astype(I32), zero, zero, zero, zero], axis=0)
    wb0_ref[...] = jnp.transpose(jnp.broadcast_to(w1, (LANES, tm)))
    wb1_ref[...] = jnp.transpose(jnp.broadcast_to(w2, (LANES, tm)))


def _router(x, rwh, rwl, rb, tri, tm):
    t = x.shape[0]
    return pl.pallas_call(
        functools.partial(_router_body, tm=tm),
        grid=(t // tm,),
        in_specs=[pl.BlockSpec((tm, D_MODEL), lambda i: (i, 0)),
                  _resident((D_MODEL, LANES)), _resident((D_MODEL, LANES)), _resident((1, LANES)), _resident((tm, tm))],
        out_specs=[pl.BlockSpec((SUBLANES, tm), lambda i: (0, i)),
                   pl.BlockSpec((tm, LANES), lambda i: (i, 0)),
                   pl.BlockSpec((tm, LANES), lambda i: (i, 0)),
                   pl.BlockSpec((N_EXPERTS, LANES), lambda i: (0, 0))],
        out_shape=[jax.ShapeDtypeStruct((SUBLANES, t), I32),
                   jax.ShapeDtypeStruct((t, LANES), F32),
                   jax.ShapeDtypeStruct((t, LANES), F32),
                   jax.ShapeDtypeStruct((N_EXPERTS, LANES), F32)],
        scratch_shapes=[pltpu.VMEM((N_EXPERTS, LANES), F32)],
        compiler_params=_cparams(("arbitrary",)),
        name="moe_router",
    )(x, rwh, rwl, rb, tri)


def _row_copy(src, s, dst, d, sem):
    return pltpu.make_async_copy(src.at[pl.ds(s, 1), :], dst.at[pl.ds(d, 1), :], sem)


def _rows_wait(hbm, n, sem):
    pltpu.make_async_copy(hbm.at[pl.ds(0, n), :], hbm.at[pl.ds(0, n), :], sem).wait()


def _dispatch_body(pad_lo_ref, pad_hi_ref, p0_ref, p1_ref, x_ref, xs_out, zrow, sem, zsem, *, tm):
    @pl.when(pl.program_id(0) == 0)
    def _():
        zrow[...] = jnp.zeros_like(zrow)
        for e in range(N_EXPERTS):
            def put(r, c):
                _row_copy(zrow, 0, xs_out, r, zsem).start()
                return c

            lax.fori_loop(pad_lo_ref[e], pad_hi_ref[e], put, 0)
        for e in range(N_EXPERTS):
            def got(r, c):
                _row_copy(zrow, 0, xs_out, 0, zsem).wait()
                return c

            lax.fori_loop(pad_lo_ref[e], pad_hi_ref[e], got, 0)

    def issue(g, c):
        t0 = g * SUBLANES
        for s in range(SUBLANES):
            for k, p_ref in enumerate((p0_ref, p1_ref)):
                _row_copy(x_ref.at[g], s, xs_out, p_ref[t0 + s], sem).start(priority=k)
        return c

    lax.fori_loop(0, tm // SUBLANES, issue, 0)
    _rows_wait(xs_out, 2 * tm, sem)


def _dispatch(pad_lo, pad_hi, pos0, pos1, x, n_rows, tm):
    t = x.shape[0]
    idx = pl.BlockSpec((tm,), lambda i, lo, hi: (i,), memory_space=pltpu.SMEM)
    return pl.pallas_call(
        functools.partial(_dispatch_body, tm=tm),
        grid_spec=pltpu.PrefetchScalarGridSpec(
            num_scalar_prefetch=2,
            grid=(t // tm,),
            in_specs=[idx, idx, pl.BlockSpec((tm // SUBLANES, SUBLANES, D_MODEL), lambda i, lo, hi: (i, 0, 0))],
            out_specs=pl.BlockSpec(memory_space=pl.ANY),
            scratch_shapes=[pltpu.VMEM((SUBLANES, D_MODEL), F32),
                            pltpu.SemaphoreType.DMA(()), pltpu.SemaphoreType.DMA(())]),
        out_shape=jax.ShapeDtypeStruct((n_rows, D_MODEL), F32),
        compiler_params=pltpu.CompilerParams(dimension_semantics=("arbitrary",), vmem_limit_bytes=VMEM_LIMIT,
                                             has_side_effects=True, disable_bounds_checks=True),
        name="moe_dispatch",
    )(pad_lo, pad_hi, pos0, pos1, x.reshape(t // SUBLANES, SUBLANES, D_MODEL))


def _ffn_body(te_ref, nu_ref, xs_ref, wgu_ref, wdn_ref, ys_ref):
    used = pl.program_id(0) < nu_ref[0]

    @pl.when(used)
    def _():
        h = _dot(xs_ref[...].astype(BF16), wgu_ref[0, 0])
        hh = (jax.nn.silu(h[:, :D_EXPERT]) * h[:, D_EXPERT:]).astype(BF16)
        ys_ref[...] = _dot(hh, wdn_ref[0, 0])

    @pl.when(jnp.logical_not(used))
    def _():
        ys_ref[...] = jnp.zeros_like(ys_ref)


def _ffn(tile_expert, n_used, xs, w_gu, w_down, layer, tm):
    p = xs.shape[0]
    row_tile = pl.BlockSpec((tm, D_MODEL), lambda i, te, nu: (i, 0))
    return pl.pallas_call(
        _ffn_body,
        grid_spec=pltpu.PrefetchScalarGridSpec(
            num_scalar_prefetch=2,
            grid=(p // tm,),
            in_specs=[row_tile,
                      pl.BlockSpec((1, 1, D_MODEL, 2 * D_EXPERT), lambda i, te, nu: (layer, te[i], 0, 0)),
                      pl.BlockSpec((1, 1, D_EXPERT, D_MODEL), lambda i, te, nu: (layer, te[i], 0, 0))],
            out_specs=row_tile),
        out_shape=jax.ShapeDtypeStruct((p, D_MODEL), F32),
        compiler_params=_cparams(("arbitrary",)),
        name="moe_expert_ffn",
    )(tile_expert, n_used, xs, w_gu, w_down)


def _combine_body(p0_ref, p1_ref, p0n_ref, p1n_ref, ys_hbm, wb0_hbm, wb1_hbm, x_hbm, g_ref, b_ref, *rest,
                  tm, n_t, n_first):
    o_refs = rest[:-14]
    slot_a = dict(y=rest[-14:-12], x=rest[-12], w=rest[-11:-9], sem=rest[-4], lsem=rest[-3])
    slot_b = dict(y=rest[-9:-7], x=rest[-7], w=rest[-6:-4], sem=rest[-2], lsem=rest[-1])
    i = pl.program_id(0)

    def dense_copies(tile, slot):
        r0 = pl.multiple_of(tile * tm, tm)
        pairs = ((x_hbm, slot["x"]), (wb0_hbm, slot["w"][0]), (wb1_hbm, slot["w"][1]))
        return [pltpu.make_async_copy(src.at[pl.ds(r0, tm), :], dst, slot["lsem"]) for src, dst in pairs]

    def request(tile, p_refs, slot):
        for cp in dense_copies(tile, slot):
            cp.start(priority=1)
        for t in range(tm):
            for k in range(2):
                _row_copy(ys_hbm, p_refs[k][t], slot["y"][k].at[t // SUBLANES], t % SUBLANES, slot["sem"]).start()

    def arrive(tile, slot):
        for cp in dense_copies(tile, slot):
            cp.wait()
        _rows_wait(ys_hbm, 2 * tm, slot["sem"])

    def reduce(slot):
        reps = D_MODEL // LANES
        w0 = jnp.concatenate([slot["w"][0][...]] * reps, axis=1)
        w1 = jnp.concatenate([slot["w"][1][...]] * reps, axis=1)
        y0 = slot["y"][0][...].reshape(tm, D_MODEL)
        y1 = slot["y"][1][...].reshape(tm, D_MODEL)
        out = _ln(DEEPNORM_ALPHA * slot["x"][...] + (w0 * y0 + w1 * y1), g_ref[...], b_ref[...])
        if len(o_refs) == 1:
            o_refs[0][...] = out
        else:
            @pl.when(i < n_first)
            def _():
                o_refs[0][...] = out

            @pl.when(i >= n_first)
            def _():
                o_refs[1][...] = out

    nxt = jnp.minimum(i + 1, n_t - 1)

    @pl.when(i == 0)
    def _():
        request(0, (p0_ref, p1_ref), slot_a)

    for s, (cur, other) in enumerate(((slot_a, slot_b), (slot_b, slot_a))):
        @pl.when(i % 2 == s)
        def _():
            arrive(i, cur)
            request(nxt, (p0n_ref, p1n_ref), other)
            reduce(cur)

            @pl.when(i == n_t - 1)
            def _():
                arrive(nxt, other)


def _combine(pos0, pos1, ys, wb0, wb1, x, g, b, tm, split_rows=None):
    t = x.shape[0]
    n_t = t // tm
    out_rows = (t,) if split_rows is None else (split_rows, t - split_rows)
    outs = [jax.ShapeDtypeStruct((r, D_MODEL), F32) for r in out_rows]
    vec = pl.BlockSpec((1, D_MODEL), lambda i: (0, 0))
    idx = pl.BlockSpec((tm,), lambda i: (i,), memory_space=pltpu.SMEM)
    idx_next = pl.BlockSpec((tm,), lambda i: (jnp.minimum(i + 1, n_t - 1),), memory_space=pltpu.SMEM)
    hbm = pl.BlockSpec(memory_space=pl.ANY)
    rows = pltpu.VMEM((tm // SUBLANES, SUBLANES, D_MODEL), F32)
    slot = [rows, rows, pltpu.VMEM((tm, D_MODEL), F32), pltpu.VMEM((tm, LANES), F32), pltpu.VMEM((tm, LANES), F32)]
    res = pl.pallas_call(
        functools.partial(_combine_body, tm=tm, n_t=n_t, n_first=out_rows[0] // tm),
        grid=(n_t,),
        in_specs=[idx, idx, idx_next, idx_next, hbm, hbm, hbm, hbm, vec, vec],
        out_specs=_tok_specs(outs, tm),
        scratch_shapes=slot + slot + [pltpu.SemaphoreType.DMA(())] * 4,
        out_shape=outs,
        compiler_params=pltpu.CompilerParams(dimension_semantics=("arbitrary",), vmem_limit_bytes=VMEM_LIMIT,
                                             disable_bounds_checks=True),
        name="moe_combine",
    )(pos0, pos1, pos0, pos1, ys, wb0, wb1, x, g, b)
    return res[0] if split_rows is None else tuple(res)


def _moe_layer(x, rwh, rwl, rb, tri, w_gu, w_down, layer, g, b, tm, tme, split_rows=None):
    t = x.shape[0]
    n_tiles = (2 * t) // tme + N_EXPERTS
    ert, wb0, wb1, cnt = _router(x, rwh, rwl, rb, tri, tm)
    counts = cnt[:, 0].astype(I32)
    padded = ((counts + tme - 1) // tme) * tme
    ends = jnp.cumsum(padded).astype(I32)
    off = ends - padded
    n_used = ends[-1:] // tme
    tile_start = jnp.arange(n_tiles, dtype=I32) * tme
    tile_expert = jnp.minimum(jnp.sum(tile_start[:, None] >= ends[None, :], axis=1), N_EXPERTS - 1).astype(I32)
    is_e = ert[0:2, :, None] == jnp.arange(N_EXPERTS, dtype=I32)
    pos = jnp.sum(jnp.where(is_e, off, 0), axis=-1) + ert[2:4]
    pad_hi = ends.at[N_EXPERTS - 1].set(n_tiles * tme)
    xs = _dispatch(off + counts, pad_hi, pos[0], pos[1], x, n_tiles * tme, tm)
    ys = _ffn(tile_expert, n_used, xs, w_gu, w_down, layer, tme)
    return _combine(pos[0], pos[1], ys, wb0, wb1, x, g, b, tm, split_rows)


def _trunk(xs, seq_lens, tm, tme, tt, ln_g, ln_b, a_w_in, a_vn_g, a_vn_b, a_w_s, a_b_s, a_w_out,
           b_w_in, b_conv_w, b_conv_b, b_w_gates, b_b_gates, b_lambda, b_w_out,
           router_w, router_b, moe_w_gu, moe_w_down):
    row = lambda v: v.reshape(1, -1).astype(F32)
    rw = jnp.pad(router_w.astype(F32), ((0, 0), (0, LANES - N_EXPERTS)))
    rwh = rw.astype(BF16)
    rwl = (rw - rwh.astype(F32)).astype(BF16)
    rb = jnp.pad(router_b.astype(F32), (0, LANES - N_EXPERTS)).reshape(1, LANES)
    tri = (jnp.arange(tm)[:, None] < jnp.arange(tm)[None, :]).astype(BF16)
    w_gu, w_down = moe_w_gu.astype(BF16), moe_w_down.astype(BF16)
    ia = ib = 0
    x = None
    for layer in range(DEPTH):
        xin = xs if x is None else (x,)
        last = layer == DEPTH - 1
        g0, b0 = row(ln_g[layer, 0]), row(ln_b[layer, 0])
        if layer % 2 == 0:
            bs_full = jnp.repeat(a_b_s[ia].T.astype(F32), CHUNK, axis=1)
            y = _a1(xin, a_w_in[ia].astype(BF16), row(a_vn_g[ia]), row(a_vn_b[ia]), a_w_s[ia].astype(BF16), bs_full, tm)
            x = _out_proj(_op_a_body, [y], a_w_out[ia].astype(BF16), xin, g0, b0, tm, "op_a")
            ia += 1
        else:
            gg, xc = _b1(x, b_w_in[ib].astype(BF16), b_conv_w[ib].astype(F32), row(b_conv_b[ib]), seq_lens, tm)
            wgt = (0.5 * b_w_gates[ib]).astype(BF16)
            wg = jnp.stack([jnp.concatenate([wgt[0], wgt[1]], axis=-1),
                            jnp.concatenate([wgt[2], wgt[3]], axis=-1)])
            hf, hb = _b2(xc, wg, 0.5 * b_b_gates[ib].astype(F32), b_lambda[ib].astype(F32), seq_lens, tt)
            x = _out_proj(_op_b_body, [hf, hb, gg], b_w_out[ib].astype(BF16), (x,), g0, b0, tm, "op_b")
            ib += 1
        x = _moe_layer(x, rwh, rwl, rb, tri, w_gu, w_down, layer,
                       row(ln_g[layer, 1]), row(ln_b[layer, 1]), tm, tme,
                       split_rows=xs[0].shape[0] if last else None)
    return x


def kernel(x_prompt, x_sample, ln_g, ln_b, a_w_in, a_vn_g, a_vn_b, a_w_s, a_b_s, a_w_out, b_w_in, b_conv_w, b_conv_b,
           b_w_gates, b_b_gates, b_lambda, b_w_out, router_w, router_b, moe_w_gu, moe_w_down):
    d = x_prompt.shape[-1]
    seq_lens = (x_prompt.shape[1],) * x_prompt.shape[0] + (x_sample.shape[1],) * x_sample.shape[0]
    xs = (x_prompt.reshape(-1, d), x_sample.reshape(-1, d))
    y_p, y_s = _trunk(xs, seq_lens, TM, TME, TT, ln_g, ln_b, a_w_in, a_vn_g, a_vn_b, a_w_s, a_b_s, a_w_out,
                      b_w_in, b_conv_w, b_conv_b, b_w_gates, b_b_gates, b_lambda, b_w_out,
                      router_w, router_b, moe_w_gu, moe_w_down)
    return y_p.reshape(x_prompt.shape), y_s.reshape(x_sample.shape)
```

```python
import functools

import jax
import jax.numpy as jnp
from jax import lax
from jax.experimental import pallas as pl
from jax.experimental.pallas import tpu as pltpu

F32 = jnp.float32
BF16 = jnp.bfloat16
I32 = jnp.int32

D_MODEL = 2048
DEPTH = 4
CHUNK = 128
A_HALF = D_MODEL
A_GROUPS = 16
RNN_WIDTH = D_MODEL
RNN_BLOCKS = 16
RNN_BDIM = RNN_WIDTH // RNN_BLOCKS
LRU_C = 8.0
N_EXPERTS = 16
N_GROUPS = 4
EXPERTS_PER_GROUP = N_EXPERTS // N_GROUPS
D_EXPERT = D_MODEL // 2
DEEPNORM_ALPHA = (2.0 * DEPTH) ** 0.25
LN_EPS = 1e-5

LANES = 128
SUBLANES = 8
VMEM_LIMIT = 56 * 1024 * 1024

TM = 512
TME = 512
TT = 256


def _cparams(sem):
    return pltpu.CompilerParams(dimension_semantics=sem, vmem_limit_bytes=VMEM_LIMIT)


def _resident(shape):
    nd = len(shape)
    return pl.BlockSpec(shape, lambda *_: (0,) * nd, pipeline_mode=pl.Buffered(1))


def _dot(a, b):
    return jnp.dot(a, b, preferred_element_type=F32)


def _ln(x, g, b):
    mu = jnp.mean(x, axis=-1, keepdims=True)
    xc = x - mu
    var = jnp.mean(xc * xc, axis=-1, keepdims=True)
    return xc * lax.rsqrt(var + LN_EPS) * g + b


def _tok_specs(pieces, tm):
    if len(pieces) == 1:
        return [pl.BlockSpec((tm, D_MODEL), lambda i: (i, 0))]
    n_first = pieces[0].shape[0] // tm
    return [pl.BlockSpec((tm, D_MODEL), lambda i: (jnp.minimum(i, n_first - 1), 0)),
            pl.BlockSpec((tm, D_MODEL), lambda i: (jnp.maximum(i - n_first, 0), 0))]


def _tok_tile(refs, n_first):
    if len(refs) == 1:
        return refs[0][...]
    return jnp.where(pl.program_id(0) < n_first, refs[0][...], refs[1][...])


def _a1_body(*refs, tm, n_x, n_first):
    w_ref, vg_ref, vb_ref, ws_ref, bs_ref, y_ref, s_scr = refs[n_x:]
    xb = _tok_tile(refs[:n_x], n_first).astype(BF16)
    v = jax.nn.gelu(_dot(xb, w_ref[:, A_HALF:]))
    v = _ln(v, vg_ref[...], vb_ref[...]).astype(BF16)
    for c in range(tm // CHUNK):
        rows = slice(c * CHUNK, (c + 1) * CHUNK)
        for g in range(A_GROUPS):
            cols = slice(g * LANES, (g + 1) * LANES)
            s_scr[rows, cols] = _dot(ws_ref[g], v[rows, cols]) + bs_ref[:, cols]
    u = jax.nn.gelu(_dot(xb, w_ref[:, :A_HALF]))
    y_ref[...] = (u * s_scr[...]).astype(BF16)


def _a1(xs, w_in, vn_g, vn_b, w_s, bs_full, tm):
    t = sum(x.shape[0] for x in xs)
    return pl.pallas_call(
        functools.partial(_a1_body, tm=tm, n_x=len(xs), n_first=xs[0].shape[0] // tm),
        grid=(t // tm,),
        in_specs=_tok_specs(xs, tm) + [
            _resident((D_MODEL, 2 * A_HALF)),
            _resident((1, A_HALF)),
            _resident((1, A_HALF)),
            _resident((A_GROUPS, CHUNK, CHUNK)),
            _resident((CHUNK, A_HALF)),
        ],
        out_specs=pl.BlockSpec((tm, A_HALF), lambda i: (i, 0)),
        out_shape=jax.ShapeDtypeStruct((t, A_HALF), BF16),
        scratch_shapes=[pltpu.VMEM((tm, A_HALF), F32)],
        compiler_params=_cparams(("arbitrary",)),
        name="a1_gmlp_front",
    )(*xs, w_in, vn_g, vn_b, w_s, bs_full)


def _half_rows(tm):
    return [slice(0, tm // 2), slice(tm // 2, tm)]


def _op_a_body(*refs, tm, n_x, n_first):
    y_ref, w_ref = refs[:2]
    g_ref, b_ref, o_ref = refs[2 + n_x:]
    x = _tok_tile(refs[2:2 + n_x], n_first)
    for rows in _half_rows(tm):
        m = _dot(y_ref[rows, :], w_ref[...])
        o_ref[rows, :] = _ln(DEEPNORM_ALPHA * x[rows, :] + m, g_ref[...], b_ref[...])


def _op_b_body(*refs, tm, n_x, n_first):
    hf_ref, hb_ref, gg_ref, w_ref = refs[:4]
    g_ref, b_ref, o_ref = refs[4 + n_x:]
    x = _tok_tile(refs[4:4 + n_x], n_first)
    for rows in _half_rows(tm):
        y = ((hf_ref[rows, :] + hb_ref[rows, :]) * gg_ref[rows, :].astype(F32)).astype(BF16)
        m = _dot(y, w_ref[...])
        o_ref[rows, :] = _ln(DEEPNORM_ALPHA * x[rows, :] + m, g_ref[...], b_ref[...])


def _out_proj(body, acts, w_out, xs, g, b, tm, name):
    t = sum(x.shape[0] for x in xs)
    tile = pl.BlockSpec((tm, D_MODEL), lambda i: (i, 0))
    return pl.pallas_call(
        functools.partial(body, tm=tm, n_x=len(xs), n_first=xs[0].shape[0] // tm),
        grid=(t // tm,),
        in_specs=([tile] * len(acts) + [_resident(w_out.shape)] + _tok_specs(xs, tm)
                  + [_resident((1, D_MODEL)), _resident((1, D_MODEL))]),
        out_specs=tile,
        out_shape=jax.ShapeDtypeStruct((t, D_MODEL), F32),
        compiler_params=_cparams(("arbitrary",)),
        name=name,
    )(*acts, w_out, *xs, g, b)


def _edge_hits(idx, marks):
    r = idx == marks[0]
    for m in marks[1:]:
        r = jnp.logical_or(r, idx == m)
    return r


def _seq_edge_tiles(seq_lens, tile):
    bounds = [0]
    for s in seq_lens:
        bounds.append(bounds[-1] + s)
    return tuple(b // tile for b in bounds[:-1]), tuple(b // tile - 1 for b in bounds[1:])


def _b1_body(x_ref, xp_ref, xn_ref, w_ref, cw_ref, cb_ref, gg_ref, xc_ref, *, tm, starts, ends):
    i = pl.program_id(0)
    prev = jnp.where(_edge_hits(i, starts), 0.0, xp_ref[...])
    nxt = jnp.where(_edge_hits(i, ends), 0.0, xn_ref[...])
    xe = jnp.concatenate([prev, x_ref[...], nxt], axis=0).astype(BF16)
    rows = slice(SUBLANES, SUBLANES + tm)
    n = tm + 2 * SUBLANES
    width = 512
    for q in range(RNN_WIDTH // width):
        cols = slice(q * width, (q + 1) * width)
        gg_ref[:, cols] = jax.nn.gelu(_dot(xe, w_ref[:, q * width:(q + 1) * width])[rows, :]).astype(BF16)
        xr = _dot(xe, w_ref[:, RNN_WIDTH + q * width:RNN_WIDTH + (q + 1) * width])
        xc_ref[:, cols] = (cb_ref[:, cols] + cw_ref[0:1, cols] * pltpu.roll(xr, 1, axis=0)[rows, :]
                           + cw_ref[1:2, cols] * xr[rows, :]
                           + cw_ref[2:3, cols] * pltpu.roll(xr, n - 1, axis=0)[rows, :]
                           + cw_ref[3:4, cols] * pltpu.roll(xr, n - 2, axis=0)[rows, :])


def _b1(x, w_in, conv_w, conv_b, seq_lens, tm):
    t = x.shape[0]
    hp = tm // SUBLANES
    n_h = t // SUBLANES
    starts, ends = _seq_edge_tiles(seq_lens, tm)
    tile = pl.BlockSpec((tm, D_MODEL), lambda i: (i, 0))
    prev = pl.BlockSpec((SUBLANES, D_MODEL), lambda i: (jnp.maximum(i * hp - 1, 0), 0))
    nxt = pl.BlockSpec((SUBLANES, D_MODEL), lambda i: (jnp.minimum((i + 1) * hp, n_h - 1), 0))
    return pl.pallas_call(
        functools.partial(_b1_body, tm=tm, starts=starts, ends=ends),
        grid=(t // tm,),
        in_specs=[tile, prev, nxt, _resident((D_MODEL, 2 * RNN_WIDTH)), _resident((4, RNN_WIDTH)), _resident((1, RNN_WIDTH))],
        out_specs=[tile, tile],
        out_shape=[jax.ShapeDtypeStruct((t, RNN_WIDTH), BF16), jax.ShapeDtypeStruct((t, RNN_WIDTH), F32)],
        compiler_params=_cparams(("arbitrary",)),
        name="b1_rglru_front",
    )(x, x, x, w_in, conv_w, conv_b)


def _scan_rows(a_s, b_s, o_ref, carry_scr, *, tt, reverse):
    n_grp = tt // SUBLANES
    width = 512
    row_id = lax.broadcasted_iota(I32, (SUBLANES, width), 0)

    def step(k, carry):
        g = (n_grp - 1 - k) if reverse else k
        r0 = pl.multiple_of(g * SUBLANES, SUBLANES)
        new = []
        for q in range(RNN_WIDTH // width):
            cols = slice(q * width, (q + 1) * width)
            a = a_s[pl.ds(r0, SUBLANES), cols]
            b = b_s[pl.ds(r0, SUBLANES), cols]
            for d in (1, 2, 4):
                if reverse:
                    keep = row_id < SUBLANES - d
                    sh = SUBLANES - d
                else:
                    keep = row_id >= d
                    sh = d
                a_sh = jnp.where(keep, pltpu.roll(a, sh, axis=0), 1.0)
                b_sh = jnp.where(keep, pltpu.roll(b, sh, axis=0), 0.0)
                b = a * b_sh + b
                a = a * a_sh
            h = a * carry[q] + b
            o_ref[pl.ds(r0, SUBLANES), cols] = h
            edge = h[0:1, :] if reverse else h[SUBLANES - 1:SUBLANES, :]
            new.append(jnp.broadcast_to(edge, (SUBLANES, width)))
        return tuple(new)

    init = tuple(carry_scr[:, q * width:(q + 1) * width] for q in range(RNN_WIDTH // width))
    fin = lax.fori_loop(0, n_grp, step, init)
    for q in range(RNN_WIDTH // width):
        carry_scr[:, q * width:(q + 1) * width] = fin[q]


def _b2_direction(d, x_ref, reset, wg_ref, bg_ref, sp_ref, o_ref, carry_scr, a_s, b_s, *, tt):
    for h in range(RNN_BLOCKS):
        cols = slice(h * RNN_BDIM, (h + 1) * RNN_BDIM)
        xh = x_ref[:, cols]
        g2 = _dot(xh.astype(BF16), wg_ref[d, h])
        t_r = jnp.tanh(g2[:, :RNN_BDIM] + bg_ref[2 * d:2 * d + 1, cols])
        t_i = jnp.tanh(g2[:, RNN_BDIM:] + bg_ref[2 * d + 1:2 * d + 2, cols])
        c2 = sp_ref[d:d + 1, cols]
        log_a = c2 * t_r + c2
        a = jnp.exp(log_a)
        mult = jnp.sqrt(-jnp.tanh(log_a) * (a * a + 1.0))
        a_s[:, cols] = a
        b_s[:, cols] = mult * ((0.5 * t_i + 0.5) * xh)

    @pl.when(reset)
    def _():
        carry_scr[...] = jnp.zeros_like(carry_scr)

    _scan_rows(a_s, b_s, o_ref, carry_scr, tt=tt, reverse=(d == 1))


def _b2_body(xf_ref, xb_ref, wg_ref, bg_ref, lam_ref, hf_ref, hb_ref, cf_scr, cbk_scr, sp_scr, a_s, b_s,
             *, tt, n_t, starts, ends):
    i = pl.program_id(0)
    j = n_t - 1 - i
    sp_scr[...] = (-0.5 * LRU_C) * jax.nn.softplus(-lam_ref[...])
    common = dict(wg_ref=wg_ref, bg_ref=bg_ref, sp_ref=sp_scr, a_s=a_s, b_s=b_s, tt=tt)
    _b2_direction(0, xf_ref, _edge_hits(i, starts), o_ref=hf_ref, carry_scr=cf_scr, **common)
    _b2_direction(1, xb_ref, _edge_hits(j, ends), o_ref=hb_ref, carry_scr=cbk_scr, **common)


def _b2(xc, wg, b_gates, lam, seq_lens, tt):
    t = xc.shape[0]
    n_t = t // tt
    starts, ends = _seq_edge_tiles(seq_lens, tt)
    w = RNN_WIDTH
    main_f = pl.BlockSpec((tt, w), lambda i: (i, 0))
    main_b = pl.BlockSpec((tt, w), lambda i: (n_t - 1 - i, 0))
    return pl.pallas_call(
        functools.partial(_b2_body, tt=tt, n_t=n_t, starts=starts, ends=ends),
        grid=(n_t,),
        in_specs=[main_f, main_b, _resident(wg.shape), _resident((4, w)), _resident((2, w))],
        out_specs=[main_f, main_b],
        out_shape=[jax.ShapeDtypeStruct((t, w), F32), jax.ShapeDtypeStruct((t, w), F32)],
        scratch_shapes=[pltpu.VMEM((SUBLANES, w), F32), pltpu.VMEM((SUBLANES, w), F32), pltpu.VMEM((2, w), F32),
                        pltpu.VMEM((tt, w), F32), pltpu.VMEM((tt, w), F32)],
        compiler_params=_cparams(("arbitrary",)),
        name="b2_gates_scan",
    )(xc, xc, wg, b_gates, lam)


def _top2_of4(a):
    m1 = jnp.maximum(jnp.maximum(a[0], a[1]), jnp.maximum(a[2], a[3]))
    i1 = jnp.where(a[0] == m1, 0, jnp.where(a[1] == m1, 1, jnp.where(a[2] == m1, 2, 3)))
    b = [jnp.where(i1 == k, -1.0, a[k]) for k in range(4)]
    m2 = jnp.maximum(jnp.maximum(b[0], b[1]), jnp.maximum(b[2], b[3]))
    i2 = jnp.where(b[0] == m2, 0, jnp.where(b[1] == m2, 1, jnp.where(b[2] == m2, 2, 3)))
    return m1, i1, m2, i2


def _router_body(x_ref, whl_ref, rb_ref, tri_ref, ert_ref, wb0_ref, wb1_ref, cnt_ref, base_scr, *, tm):
    @pl.when(pl.program_id(0) == 0)
    def _():
        base_scr[...] = jnp.zeros_like(base_scr)

    x = x_ref[...]
    xh = x.astype(BF16)
    xl = (x - xh.astype(F32)).astype(BF16)
    z = _dot(xh, whl_ref[...])
    logits = z[:, :LANES] + z[:, LANES:] + _dot(xl, whl_ref[:, :LANES]) + rb_ref[...]
    lt = jnp.transpose(logits)
    l = [lt[e:e + 1, :] for e in range(N_EXPERTS)]
    mx = l[0]
    for e in range(1, N_EXPERTS):
        mx = jnp.maximum(mx, l[e])
    ex = [jnp.exp(v - mx) for v in l]
    den = ex[0]
    for e in range(1, N_EXPERTS):
        den = den + ex[e]
    p = [v / den for v in ex]

    tops = [_top2_of4(p[g * EXPERTS_PER_GROUP:(g + 1) * EXPERTS_PER_GROUP]) for g in range(N_GROUPS)]
    score = [t[0] + t[2] for t in tops]
    best = jnp.maximum(jnp.maximum(score[0], score[1]), jnp.maximum(score[2], score[3]))
    gsel = jnp.where(score[0] == best, 0, jnp.where(score[1] == best, 1, jnp.where(score[2] == best, 2, 3)))

    def pick(field):
        return jnp.where(gsel == 0, tops[0][field],
                         jnp.where(gsel == 1, tops[1][field], jnp.where(gsel == 2, tops[2][field], tops[3][field])))

    v1, i1, v2, i2 = pick(0), pick(1), pick(2), pick(3)
    e1 = gsel * EXPERTS_PER_GROUP + i1
    e2 = gsel * EXPERTS_PER_GROUP + i2
    vs = v1 + v2
    w1 = v1 / vs
    w2 = v2 / vs

    eid = lax.broadcasted_iota(I32, (N_EXPERTS, tm), 0)
    hit1 = eid == e1
    hit2 = eid == e2
    oh = jnp.where(hit1, 1.0, 0.0) + jnp.where(hit2, 1.0, 0.0)
    before = _dot(oh.astype(BF16), tri_ref[...])
    tot = before + jnp.concatenate([base_scr[...]] * (tm // LANES), axis=1)
    r1 = jnp.sum(jnp.where(hit1, tot, 0.0), axis=0, keepdims=True)
    r2 = jnp.sum(jnp.where(hit2, tot, 0.0), axis=0, keepdims=True)
    base_scr[...] = base_scr[...] + jnp.sum(oh, axis=1, keepdims=True)
    cnt_ref[...] = base_scr[...]

    zero = jnp.zeros((1, tm), I32)
    ert_ref[...] = jnp.concatenate([e1, e2, r1.astype(I32), r2.astype(I32), zero, zero, zero, zero], axis=0)
    wb0_ref[...] = jnp.transpose(jnp.broadcast_to(w1, (LANES, tm)))
    wb1_ref[...] = jnp.transpose(jnp.broadcast_to(w2, (LANES, tm)))


def _router(x, rwhl, rb, tri, tm):
    t = x.shape[0]
    return pl.pallas_call(
        functools.partial(_router_body, tm=tm),
        grid=(t // tm,),
        in_specs=[pl.BlockSpec((tm, D_MODEL), lambda i: (i, 0)),
                  _resident((D_MODEL, 2 * LANES)), _resident((1, LANES)), _resident((tm, tm))],
        out_specs=[pl.BlockSpec((SUBLANES, tm), lambda i: (0, i)),
                   pl.BlockSpec((tm, LANES), lambda i: (i, 0)),
                   pl.BlockSpec((tm, LANES), lambda i: (i, 0)),
                   pl.BlockSpec((N_EXPERTS, LANES), lambda i: (0, 0))],
        out_shape=[jax.ShapeDtypeStruct((SUBLANES, t), I32),
                   jax.ShapeDtypeStruct((t, LANES), F32),
                   jax.ShapeDtypeStruct((t, LANES), F32),
                   jax.ShapeDtypeStruct((N_EXPERTS, LANES), F32)],
        scratch_shapes=[pltpu.VMEM((N_EXPERTS, LANES), F32)],
        compiler_params=_cparams(("arbitrary",)),
        name="moe_router",
    )(x, rwhl, rb, tri)


def _row_copy(src, s, dst, d, sem):
    return pltpu.make_async_copy(src.at[pl.ds(s, 1), :], dst.at[pl.ds(d, 1), :], sem)


def _rows_wait(hbm, n, sem):
    pltpu.make_async_copy(hbm.at[pl.ds(0, n), :], hbm.at[pl.ds(0, n), :], sem).wait()


DISPATCH_SLOTS = 3


def _dispatch_body(pad_lo_ref, pad_hi_ref, p0_ref, p1_ref, x_hbm, xs_out, xbuf, zrow, in_sem, out_sem, zsem,
                   *, tm, n_t):
    i = pl.program_id(0)
    groups = tm // SUBLANES

    def tile_in(tile):
        g0 = pl.multiple_of(tile * groups, groups)
        slot = tile % DISPATCH_SLOTS
        return pltpu.make_async_copy(x_hbm.at[pl.ds(g0, groups)], xbuf.at[slot], in_sem.at[slot])

    @pl.when(i == 0)
    def _():
        tile_in(0).start()
        if n_t > 1:
            tile_in(1).start()
        zrow[...] = jnp.zeros_like(zrow)
        for e in range(N_EXPERTS):
            def put(r, c):
                _row_copy(zrow, 0, xs_out, r, zsem).start()
                return c

            lax.fori_loop(pad_lo_ref[e], pad_hi_ref[e], put, 0)
        for e in range(N_EXPERTS):
            def got(r, c):
                _row_copy(zrow, 0, xs_out, 0, zsem).wait()
                return c

            lax.fori_loop(pad_lo_ref[e], pad_hi_ref[e], got, 0)

    tile_in(i).wait()
    src_tile = xbuf.at[i % DISPATCH_SLOTS]
    sem = out_sem.at[i % 2]

    def issue(g, c):
        t0 = g * SUBLANES
        for s in range(SUBLANES):
            for k, p_ref in enumerate((p0_ref, p1_ref)):
                _row_copy(src_tile.at[g], s, xs_out, p_ref[t0 + s], sem).start(priority=k)
        return c

    lax.fori_loop(0, groups, issue, 0)

    @pl.when(i > 0)
    def _():
        _rows_wait(xs_out, 2 * tm, out_sem.at[(i + 1) % 2])

    @pl.when(i + 2 < n_t)
    def _():
        tile_in(i + 2).start()

    @pl.when(i == n_t - 1)
    def _():
        _rows_wait(xs_out, 2 * tm, sem)


def _dispatch(pad_lo, pad_hi, pos0, pos1, x, n_rows, tm):
    t = x.shape[0]
    n_t = t // tm
    idx = pl.BlockSpec((tm,), lambda i, lo, hi: (i,), memory_space=pltpu.SMEM)
    return pl.pallas_call(
        functools.partial(_dispatch_body, tm=tm, n_t=n_t),
        grid_spec=pltpu.PrefetchScalarGridSpec(
            num_scalar_prefetch=2,
            grid=(n_t,),
            in_specs=[idx, idx, pl.BlockSpec(memory_space=pl.ANY)],
            out_specs=pl.BlockSpec(memory_space=pl.ANY),
            scratch_shapes=[pltpu.VMEM((DISPATCH_SLOTS, tm // SUBLANES, SUBLANES, D_MODEL), F32),
                            pltpu.VMEM((SUBLANES, D_MODEL), F32),
                            pltpu.SemaphoreType.DMA((DISPATCH_SLOTS,)), pltpu.SemaphoreType.DMA((2,)),
                            pltpu.SemaphoreType.DMA(())]),
        out_shape=jax.ShapeDtypeStruct((n_rows, D_MODEL), F32),
        compiler_params=pltpu.CompilerParams(dimension_semantics=("arbitrary",), vmem_limit_bytes=VMEM_LIMIT,
                                             has_side_effects=True, disable_bounds_checks=True),
        name="moe_dispatch",
    )(pad_lo, pad_hi, pos0, pos1, x.reshape(t // SUBLANES, SUBLANES, D_MODEL))


def _ffn_body(te_ref, nu_ref, xs_ref, wgu_ref, wdn_ref, ys_ref):
    used = pl.program_id(0) < nu_ref[0]

    @pl.when(used)
    def _():
        h = _dot(xs_ref[...].astype(BF16), wgu_ref[0, 0])
        hh = (jax.nn.silu(h[:, :D_EXPERT]) * h[:, D_EXPERT:]).astype(BF16)
        ys_ref[...] = _dot(hh, wdn_ref[0, 0])

    @pl.when(jnp.logical_not(used))
    def _():
        ys_ref[...] = jnp.zeros_like(ys_ref)


def _ffn(tile_expert, n_used, xs, w_gu, w_down, layer, tm):
    p = xs.shape[0]
    row_tile = pl.BlockSpec((tm, D_MODEL), lambda i, te, nu: (i, 0))
    return pl.pallas_call(
        _ffn_body,
        grid_spec=pltpu.PrefetchScalarGridSpec(
            num_scalar_prefetch=2,
            grid=(p // tm,),
            in_specs=[row_tile,
                      pl.BlockSpec((1, 1, D_MODEL, 2 * D_EXPERT), lambda i, te, nu: (layer, te[i], 0, 0)),
                      pl.BlockSpec((1, 1, D_EXPERT, D_MODEL), lambda i, te, nu: (layer, te[i], 0, 0))],
            out_specs=row_tile),
        out_shape=jax.ShapeDtypeStruct((p, D_MODEL), F32),
        compiler_params=_cparams(("arbitrary",)),
        name="moe_expert_ffn",
    )(tile_expert, n_used, xs, w_gu, w_down)


def _combine_body(p0_ref, p1_ref, p0n_ref, p1n_ref, ys_hbm, wb0_ref, wb1_ref, x_ref, g_ref, b_ref, *rest,
                  tm, n_t, n_first):
    o_refs = rest[:-6]
    a0, a1, b0, b1, sem_a, sem_b = rest[-6:]
    i = pl.program_id(0)

    def gather(p_refs, bufs, sem):
        def issue(g, c):
            t0 = g * SUBLANES
            for s in range(SUBLANES):
                for k in range(2):
                    _row_copy(ys_hbm, p_refs[k][t0 + s], bufs[k].at[g], s, sem).start(priority=k)
            return c

        lax.fori_loop(0, tm // SUBLANES, issue, 0)

    def reduce(bufs, sem):
        _rows_wait(ys_hbm, 2 * tm, sem)
        reps = D_MODEL // LANES
        w0 = jnp.concatenate([wb0_ref[...]] * reps, axis=1)
        w1 = jnp.concatenate([wb1_ref[...]] * reps, axis=1)
        y0 = bufs[0][...].reshape(tm, D_MODEL)
        y1 = bufs[1][...].reshape(tm, D_MODEL)
        out = _ln(DEEPNORM_ALPHA * x_ref[...] + (w0 * y0 + w1 * y1), g_ref[...], b_ref[...])
        if len(o_refs) == 1:
            o_refs[0][...] = out
        else:
            @pl.when(i < n_first)
            def _():
                o_refs[0][...] = out

            @pl.when(i >= n_first)
            def _():
                o_refs[1][...] = out

    slots = (((a0, a1), sem_a), ((b0, b1), sem_b))

    @pl.when(i == 0)
    def _():
        gather((p0_ref, p1_ref), *slots[0])

    for s in range(2):
        @pl.when(i % 2 == s)
        def _():
            @pl.when(i + 1 < n_t)
            def _():
                gather((p0n_ref, p1n_ref), *slots[1 - s])

            reduce(*slots[s])


def _combine(pos0, pos1, ys, wb0, wb1, x, g, b, tm, split_rows=None):
    t = x.shape[0]
    n_t = t // tm
    out_rows = (t,) if split_rows is None else (split_rows, t - split_rows)
    outs = [jax.ShapeDtypeStruct((r, D_MODEL), F32) for r in out_rows]
    tile = pl.BlockSpec((tm, D_MODEL), lambda i: (i, 0))
    wtile = pl.BlockSpec((tm, LANES), lambda i: (i, 0))
    vec = pl.BlockSpec((1, D_MODEL), lambda i: (0, 0))
    idx = pl.BlockSpec((tm,), lambda i: (i,), memory_space=pltpu.SMEM)
    idx_next = pl.BlockSpec((tm,), lambda i: (jnp.minimum(i + 1, n_t - 1),), memory_space=pltpu.SMEM)
    rows = pltpu.VMEM((tm // SUBLANES, SUBLANES, D_MODEL), F32)
    res = pl.pallas_call(
        functools.partial(_combine_body, tm=tm, n_t=n_t, n_first=out_rows[0] // tm),
        grid=(n_t,),
        in_specs=[idx, idx, idx_next, idx_next, pl.BlockSpec(memory_space=pl.ANY), wtile, wtile, tile, vec, vec],
        out_specs=_tok_specs(outs, tm),
        scratch_shapes=[rows, rows, rows, rows, pltpu.SemaphoreType.DMA(()), pltpu.SemaphoreType.DMA(())],
        out_shape=outs,
        compiler_params=pltpu.CompilerParams(dimension_semantics=("arbitrary",), vmem_limit_bytes=VMEM_LIMIT,
                                             disable_bounds_checks=True),
        name="moe_combine",
    )(pos0, pos1, pos0, pos1, ys, wb0, wb1, x, g, b)
    return res[0] if split_rows is None else tuple(res)


def _moe_layer(x, rwhl, rb, tri, w_gu, w_down, layer, g, b, tm, tme, split_rows=None):
    t = x.shape[0]
    n_tiles = (2 * t) // tme + N_EXPERTS
    ert, wb0, wb1, cnt = _router(x, rwhl, rb, tri, tm)
    counts = cnt[:, 0].astype(I32)
    padded = ((counts + tme - 1) // tme) * tme
    ends = jnp.cumsum(padded).astype(I32)
    off = ends - padded
    n_used = ends[-1:] // tme
    tile_start = jnp.arange(n_tiles, dtype=I32) * tme
    tile_expert = jnp.minimum(jnp.sum(tile_start[:, None] >= ends[None, :], axis=1), N_EXPERTS - 1).astype(I32)
    is_e = ert[0:2, :, None] == jnp.arange(N_EXPERTS, dtype=I32)
    pos = jnp.sum(jnp.where(is_e, off, 0), axis=-1) + ert[2:4]
    pad_hi = ends.at[N_EXPERTS - 1].set(n_tiles * tme)
    xs = _dispatch(off + counts, pad_hi, pos[0], pos[1], x, n_tiles * tme, tm)
    ys = _ffn(tile_expert, n_used, xs, w_gu, w_down, layer, tme)
    return _combine(pos[0], pos[1], ys, wb0, wb1, x, g, b, tm, split_rows)


def _trunk(xs, seq_lens, tm, tme, tt, ln_g, ln_b, a_w_in, a_vn_g, a_vn_b, a_w_s, a_b_s, a_w_out,
           b_w_in, b_conv_w, b_conv_b, b_w_gates, b_b_gates, b_lambda, b_w_out,
           router_w, router_b, moe_w_gu, moe_w_down):
    row = lambda v: v.reshape(1, -1).astype(F32)
    rw = jnp.pad(router_w.astype(F32), ((0, 0), (0, LANES - N_EXPERTS)))
    rwh = rw.astype(BF16)
    rwhl = jnp.concatenate([rwh, (rw - rwh.astype(F32)).astype(BF16)], axis=1)
    rb = jnp.pad(router_b.astype(F32), (0, LANES - N_EXPERTS)).reshape(1, LANES)
    tri = (jnp.arange(tm)[:, None] < jnp.arange(tm)[None, :]).astype(BF16)
    w_gu, w_down = moe_w_gu.astype(BF16), moe_w_down.astype(BF16)
    ia = ib = 0
    x = None
    for layer in range(DEPTH):
        xin = xs if x is None else (x,)
        last = layer == DEPTH - 1
        g0, b0 = row(ln_g[layer, 0]), row(ln_b[layer, 0])
        if layer % 2 == 0:
            bs_full = jnp.repeat(a_b_s[ia].T.astype(F32), CHUNK, axis=1)
            y = _a1(xin, a_w_in[ia].astype(BF16), row(a_vn_g[ia]), row(a_vn_b[ia]), a_w_s[ia].astype(BF16), bs_full, tm)
            x = _out_proj(_op_a_body, [y], a_w_out[ia].astype(BF16), xin, g0, b0, tm, "op_a")
            ia += 1
        else:
            gg, xc = _b1(x, b_w_in[ib].astype(BF16), b_conv_w[ib].astype(F32), row(b_conv_b[ib]), seq_lens, tm)
            wgt = (0.5 * b_w_gates[ib]).astype(BF16)
            wg = jnp.stack([jnp.concatenate([wgt[0], wgt[1]], axis=-1),
                            jnp.concatenate([wgt[2], wgt[3]], axis=-1)])
            hf, hb = _b2(xc, wg, 0.5 * b_b_gates[ib].astype(F32), b_lambda[ib].astype(F32), seq_lens, tt)
            x = _out_proj(_op_b_body, [hf, hb, gg], b_w_out[ib].astype(BF16), (x,), g0, b0, tm, "op_b")
            ib += 1
        x = _moe_layer(x, rwhl, rb, tri, w_gu, w_down, layer,
                       row(ln_g[layer, 1]), row(ln_b[layer, 1]), tm, tme,
                       split_rows=xs[0].shape[0] if last else None)
    return x


def kernel(x_prompt, x_sample, ln_g, ln_b, a_w_in, a_vn_g, a_vn_b, a_w_s, a_b_s, a_w_out, b_w_in, b_conv_w, b_conv_b,
           b_w_gates, b_b_gates, b_lambda, b_w_out, router_w, router_b, moe_w_gu, moe_w_down):
    d = x_prompt.shape[-1]
    seq_lens = (x_prompt.shape[1],) * x_prompt.shape[0] + (x_sample.shape[1],) * x_sample.shape[0]
    xs = (x_prompt.reshape(-1, d), x_sample.reshape(-1, d))
    y_p, y_s = _trunk(xs, seq_lens, TM, TME, TT, ln_g, ln_b, a_w_in, a_vn_g, a_vn_b, a_w_s, a_b_s, a_w_out,
                      b_w_in, b_conv_w, b_conv_b, b_w_gates, b_b_gates, b_lambda, b_w_out,
                      router_w, router_b, moe_w_gu, moe_w_down)
    return y_p.reshape(x_prompt.shape), y_s.reshape(x_sample.shape)
```

```python
import functools

import jax
import jax.numpy as jnp
from jax import lax
from jax.experimental import pallas as pl
from jax.experimental.pallas import tpu as pltpu

F32 = jnp.float32
BF16 = jnp.bfloat16
I32 = jnp.int32

D_MODEL = 2048
DEPTH = 4
CHUNK = 128
A_HALF = D_MODEL
A_GROUPS = 16
RNN_WIDTH = D_MODEL
RNN_BLOCKS = 16
RNN_BDIM = RNN_WIDTH // RNN_BLOCKS
LRU_C = 8.0
N_EXPERTS = 16
N_GROUPS = 4
EXPERTS_PER_GROUP = N_EXPERTS // N_GROUPS
D_EXPERT = D_MODEL // 2
DEEPNORM_ALPHA = (2.0 * DEPTH) ** 0.25
LN_EPS = 1e-5

LANES = 128
SUBLANES = 8
VMEM_LIMIT = 56 * 1024 * 1024

TM = 512
TME = 512
TT = 256


def _cparams(sem):
    return pltpu.CompilerParams(dimension_semantics=sem, vmem_limit_bytes=VMEM_LIMIT)


def _resident(shape):
    nd = len(shape)
    return pl.BlockSpec(shape, lambda *_: (0,) * nd, pipeline_mode=pl.Buffered(1))


def _dot(a, b):
    return jnp.dot(a, b, preferred_element_type=F32)


def _ln(x, g, b):
    mu = jnp.mean(x, axis=-1, keepdims=True)
    xc = x - mu
    var = jnp.mean(xc * xc, axis=-1, keepdims=True)
    return xc * lax.rsqrt(var + LN_EPS) * g + b


def _tok_specs(pieces, tm):
    if len(pieces) == 1:
        return [pl.BlockSpec((tm, D_MODEL), lambda i: (i, 0))]
    n_first = pieces[0].shape[0] // tm
    return [pl.BlockSpec((tm, D_MODEL), lambda i: (jnp.minimum(i, n_first - 1), 0)),
            pl.BlockSpec((tm, D_MODEL), lambda i: (jnp.maximum(i - n_first, 0), 0))]


def _tok_tile(refs, n_first):
    if len(refs) == 1:
        return refs[0][...]
    return jnp.where(pl.program_id(0) < n_first, refs[0][...], refs[1][...])


def _a1_body(*refs, tm, n_x, n_first):
    w_ref, vg_ref, vb_ref, ws_ref, bs_ref, y_ref, s_scr = refs[n_x:]
    xb = _tok_tile(refs[:n_x], n_first).astype(BF16)
    v = jax.nn.gelu(_dot(xb, w_ref[:, A_HALF:]))
    v = _ln(v, vg_ref[...], vb_ref[...]).astype(BF16)
    for c in range(tm // CHUNK):
        rows = slice(c * CHUNK, (c + 1) * CHUNK)
        for g in range(A_GROUPS):
            cols = slice(g * LANES, (g + 1) * LANES)
            s_scr[rows, cols] = _dot(ws_ref[g], v[rows, cols]) + bs_ref[:, cols]
    u = jax.nn.gelu(_dot(xb, w_ref[:, :A_HALF]))
    y_ref[...] = (u * s_scr[...]).astype(BF16)


def _a1(xs, w_in, vn_g, vn_b, w_s, bs_full, tm):
    t = sum(x.shape[0] for x in xs)
    return pl.pallas_call(
        functools.partial(_a1_body, tm=tm, n_x=len(xs), n_first=xs[0].shape[0] // tm),
        grid=(t // tm,),
        in_specs=_tok_specs(xs, tm) + [
            _resident((D_MODEL, 2 * A_HALF)),
            _resident((1, A_HALF)),
            _resident((1, A_HALF)),
            _resident((A_GROUPS, CHUNK, CHUNK)),
            _resident((CHUNK, A_HALF)),
        ],
        out_specs=pl.BlockSpec((tm, A_HALF), lambda i: (i, 0)),
        out_shape=jax.ShapeDtypeStruct((t, A_HALF), BF16),
        scratch_shapes=[pltpu.VMEM((tm, A_HALF), F32)],
        compiler_params=_cparams(("arbitrary",)),
        name="a1_gmlp_front",
    )(*xs, w_in, vn_g, vn_b, w_s, bs_full)


def _half_rows(tm):
    return [slice(0, tm // 2), slice(tm // 2, tm)]


def _op_a_body(*refs, tm, n_x, n_first):
    y_ref, w_ref = refs[:2]
    g_ref, b_ref, o_ref = refs[2 + n_x:]
    x = _tok_tile(refs[2:2 + n_x], n_first)
    for rows in _half_rows(tm):
        m = _dot(y_ref[rows, :], w_ref[...])
        o_ref[rows, :] = _ln(DEEPNORM_ALPHA * x[rows, :] + m, g_ref[...], b_ref[...])


def _op_b_body(*refs, tm, n_x, n_first):
    hf_ref, hb_ref, gg_ref, w_ref = refs[:4]
    g_ref, b_ref, o_ref = refs[4 + n_x:]
    x = _tok_tile(refs[4:4 + n_x], n_first)
    for rows in _half_rows(tm):
        y = ((hf_ref[rows, :].astype(F32) + hb_ref[rows, :].astype(F32)) * gg_ref[rows, :].astype(F32)).astype(BF16)
        m = _dot(y, w_ref[...])
        o_ref[rows, :] = _ln(DEEPNORM_ALPHA * x[rows, :] + m, g_ref[...], b_ref[...])


def _out_proj(body, acts, w_out, xs, g, b, tm, name):
    t = sum(x.shape[0] for x in xs)
    tile = pl.BlockSpec((tm, D_MODEL), lambda i: (i, 0))
    return pl.pallas_call(
        functools.partial(body, tm=tm, n_x=len(xs), n_first=xs[0].shape[0] // tm),
        grid=(t // tm,),
        in_specs=([tile] * len(acts) + [_resident(w_out.shape)] + _tok_specs(xs, tm)
                  + [_resident((1, D_MODEL)), _resident((1, D_MODEL))]),
        out_specs=tile,
        out_shape=jax.ShapeDtypeStruct((t, D_MODEL), F32),
        compiler_params=_cparams(("arbitrary",)),
        name=name,
    )(*acts, w_out, *xs, g, b)


def _edge_hits(idx, marks):
    r = idx == marks[0]
    for m in marks[1:]:
        r = jnp.logical_or(r, idx == m)
    return r


def _seq_edge_tiles(seq_lens, tile):
    bounds = [0]
    for s in seq_lens:
        bounds.append(bounds[-1] + s)
    return tuple(b // tile for b in bounds[:-1]), tuple(b // tile - 1 for b in bounds[1:])


def _b1_body(x_ref, xp_ref, xn_ref, w_ref, cw_ref, cb_ref, gg_ref, xc_ref, *, tm, starts, ends):
    i = pl.program_id(0)
    prev = jnp.where(_edge_hits(i, starts), 0.0, xp_ref[...])
    nxt = jnp.where(_edge_hits(i, ends), 0.0, xn_ref[...])
    xe = jnp.concatenate([prev, x_ref[...], nxt], axis=0).astype(BF16)
    rows = slice(SUBLANES, SUBLANES + tm)
    n = tm + 2 * SUBLANES
    width = 512
    for q in range(RNN_WIDTH // width):
        cols = slice(q * width, (q + 1) * width)
        gg_ref[:, cols] = jax.nn.gelu(_dot(xe, w_ref[:, q * width:(q + 1) * width])[rows, :]).astype(BF16)
        xr = _dot(xe, w_ref[:, RNN_WIDTH + q * width:RNN_WIDTH + (q + 1) * width])
        xc_ref[:, cols] = (cb_ref[:, cols] + cw_ref[0:1, cols] * pltpu.roll(xr, 1, axis=0)[rows, :]
                           + cw_ref[1:2, cols] * xr[rows, :]
                           + cw_ref[2:3, cols] * pltpu.roll(xr, n - 1, axis=0)[rows, :]
                           + cw_ref[3:4, cols] * pltpu.roll(xr, n - 2, axis=0)[rows, :])


def _b1(x, w_in, conv_w, conv_b, seq_lens, tm):
    t = x.shape[0]
    hp = tm // SUBLANES
    n_h = t // SUBLANES
    starts, ends = _seq_edge_tiles(seq_lens, tm)
    tile = pl.BlockSpec((tm, D_MODEL), lambda i: (i, 0))
    prev = pl.BlockSpec((SUBLANES, D_MODEL), lambda i: (jnp.maximum(i * hp - 1, 0), 0))
    nxt = pl.BlockSpec((SUBLANES, D_MODEL), lambda i: (jnp.minimum((i + 1) * hp, n_h - 1), 0))
    return pl.pallas_call(
        functools.partial(_b1_body, tm=tm, starts=starts, ends=ends),
        grid=(t // tm,),
        in_specs=[tile, prev, nxt, _resident((D_MODEL, 2 * RNN_WIDTH)), _resident((4, RNN_WIDTH)), _resident((1, RNN_WIDTH))],
        out_specs=[tile, tile],
        out_shape=[jax.ShapeDtypeStruct((t, RNN_WIDTH), BF16), jax.ShapeDtypeStruct((t, RNN_WIDTH), F32)],
        compiler_params=_cparams(("arbitrary",)),
        name="b1_rglru_front",
    )(x, x, x, w_in, conv_w, conv_b)


def _scan_rows(a_s, b_s, o_ref, carry_scr, *, tt, reverse):
    n_grp = tt // SUBLANES
    width = 512
    row_id = lax.broadcasted_iota(I32, (SUBLANES, width), 0)

    def step(k, carry):
        g = (n_grp - 1 - k) if reverse else k
        r0 = pl.multiple_of(g * SUBLANES, SUBLANES)
        new = []
        for q in range(RNN_WIDTH // width):
            cols = slice(q * width, (q + 1) * width)
            a = a_s[pl.ds(r0, SUBLANES), cols]
            b = b_s[pl.ds(r0, SUBLANES), cols]
            for d in (1, 2, 4):
                if reverse:
                    keep = row_id < SUBLANES - d
                    sh = SUBLANES - d
                else:
                    keep = row_id >= d
                    sh = d
                a_sh = jnp.where(keep, pltpu.roll(a, sh, axis=0), 1.0)
                b_sh = jnp.where(keep, pltpu.roll(b, sh, axis=0), 0.0)
                b = a * b_sh + b
                a = a * a_sh
            h = a * carry[q] + b
            o_ref[pl.ds(r0, SUBLANES), cols] = h.astype(o_ref.dtype)
            edge = h[0:1, :] if reverse else h[SUBLANES - 1:SUBLANES, :]
            new.append(jnp.broadcast_to(edge, (SUBLANES, width)))
        return tuple(new)

    init = tuple(carry_scr[:, q * width:(q + 1) * width] for q in range(RNN_WIDTH // width))
    fin = lax.fori_loop(0, n_grp, step, init)
    for q in range(RNN_WIDTH // width):
        carry_scr[:, q * width:(q + 1) * width] = fin[q]


def _b2_direction(d, x_ref, reset, wg_ref, bg_ref, sp_ref, o_ref, carry_scr, a_s, b_s, *, tt):
    for h in range(RNN_BLOCKS):
        cols = slice(h * RNN_BDIM, (h + 1) * RNN_BDIM)
        xh = x_ref[:, cols]
        g2 = _dot(xh.astype(BF16), wg_ref[d, h])
        t_r = jnp.tanh(g2[:, :RNN_BDIM] + bg_ref[2 * d:2 * d + 1, cols])
        t_i = jnp.tanh(g2[:, RNN_BDIM:] + bg_ref[2 * d + 1:2 * d + 2, cols])
        c2 = sp_ref[d:d + 1, cols]
        log_a = c2 * t_r + c2
        a = jnp.exp(log_a)
        z = -jnp.tanh(log_a) * (a * a + 1.0)
        mult = jnp.where(z > 0.0, z * lax.rsqrt(z), 0.0)
        a_s[:, cols] = a
        b_s[:, cols] = mult * ((0.5 * t_i + 0.5) * xh)

    @pl.when(reset)
    def _():
        carry_scr[...] = jnp.zeros_like(carry_scr)

    _scan_rows(a_s, b_s, o_ref, carry_scr, tt=tt, reverse=(d == 1))


def _b2_body(xf_ref, xb_ref, wg_ref, bg_ref, lam_ref, hf_ref, hb_ref, cf_scr, cbk_scr, sp_scr, a_s, b_s,
             *, tt, n_t, starts, ends):
    i = pl.program_id(0)
    j = n_t - 1 - i
    sp_scr[...] = (-0.5 * LRU_C) * jax.nn.softplus(-lam_ref[...])
    common = dict(wg_ref=wg_ref, bg_ref=bg_ref, sp_ref=sp_scr, a_s=a_s, b_s=b_s, tt=tt)
    _b2_direction(0, xf_ref, _edge_hits(i, starts), o_ref=hf_ref, carry_scr=cf_scr, **common)
    _b2_direction(1, xb_ref, _edge_hits(j, ends), o_ref=hb_ref, carry_scr=cbk_scr, **common)


def _b2(xc, wg, b_gates, lam, seq_lens, tt):
    t = xc.shape[0]
    n_t = t // tt
    starts, ends = _seq_edge_tiles(seq_lens, tt)
    w = RNN_WIDTH
    main_f = pl.BlockSpec((tt, w), lambda i: (i, 0))
    main_b = pl.BlockSpec((tt, w), lambda i: (n_t - 1 - i, 0))
    return pl.pallas_call(
        functools.partial(_b2_body, tt=tt, n_t=n_t, starts=starts, ends=ends),
        grid=(n_t,),
        in_specs=[main_f, main_b, _resident(wg.shape), _resident((4, w)), _resident((2, w))],
        out_specs=[main_f, main_b],
        out_shape=[jax.ShapeDtypeStruct((t, w), BF16), jax.ShapeDtypeStruct((t, w), BF16)],
        scratch_shapes=[pltpu.VMEM((SUBLANES, w), F32), pltpu.VMEM((SUBLANES, w), F32), pltpu.VMEM((2, w), F32),
                        pltpu.VMEM((tt, w), F32), pltpu.VMEM((tt, w), F32)],
        compiler_params=_cparams(("arbitrary",)),
        name="b2_gates_scan",
    )(xc, xc, wg, b_gates, lam)


def _top2_of4(a):
    m1 = jnp.maximum(jnp.maximum(a[0], a[1]), jnp.maximum(a[2], a[3]))
    i1 = jnp.where(a[0] == m1, 0, jnp.where(a[1] == m1, 1, jnp.where(a[2] == m1, 2, 3)))
    b = [jnp.where(i1 == k, -1.0, a[k]) for k in range(4)]
    m2 = jnp.maximum(jnp.maximum(b[0], b[1]), jnp.maximum(b[2], b[3]))
    i2 = jnp.where(b[0] == m2, 0, jnp.where(b[1] == m2, 1, jnp.where(b[2] == m2, 2, 3)))
    return m1, i1, m2, i2


def _router_body(x_ref, whl_ref, rb_ref, tri_ref, ert_ref, wb0_ref, wb1_ref, cnt_ref, base_scr, *, tm):
    @pl.when(pl.program_id(0) == 0)
    def _():
        base_scr[...] = jnp.zeros_like(base_scr)

    x = x_ref[...]
    xh = x.astype(BF16)
    xl = (x - xh.astype(F32)).astype(BF16)
    z = _dot(xh, whl_ref[...])
    logits = z[:, :LANES] + z[:, LANES:] + _dot(xl, whl_ref[:, :LANES]) + rb_ref[...]
    lt = jnp.transpose(logits)
    l = [lt[e:e + 1, :] for e in range(N_EXPERTS)]
    mx = l[0]
    for e in range(1, N_EXPERTS):
        mx = jnp.maximum(mx, l[e])
    ex = [jnp.exp(v - mx) for v in l]
    den = ex[0]
    for e in range(1, N_EXPERTS):
        den = den + ex[e]
    p = [v / den for v in ex]

    tops = [_top2_of4(p[g * EXPERTS_PER_GROUP:(g + 1) * EXPERTS_PER_GROUP]) for g in range(N_GROUPS)]
    score = [t[0] + t[2] for t in tops]
    best = jnp.maximum(jnp.maximum(score[0], score[1]), jnp.maximum(score[2], score[3]))
    gsel = jnp.where(score[0] == best, 0, jnp.where(score[1] == best, 1, jnp.where(score[2] == best, 2, 3)))

    def pick(field):
        return jnp.where(gsel == 0, tops[0][field],
                         jnp.where(gsel == 1, tops[1][field], jnp.where(gsel == 2, tops[2][field], tops[3][field])))

    v1, i1, v2, i2 = pick(0), pick(1), pick(2), pick(3)
    e1 = gsel * EXPERTS_PER_GROUP + i1
    e2 = gsel * EXPERTS_PER_GROUP + i2
    vs = v1 + v2
    w1 = v1 / vs
    w2 = v2 / vs

    eid = lax.broadcasted_iota(I32, (N_EXPERTS, tm), 0)
    hit1 = eid == e1
    hit2 = eid == e2
    oh = jnp.where(hit1, 1.0, 0.0) + jnp.where(hit2, 1.0, 0.0)
    before = _dot(oh.astype(BF16), tri_ref[...])
    tot = before + jnp.concatenate([base_scr[...]] * (tm // LANES), axis=1)
    r1 = jnp.sum(jnp.where(hit1, tot, 0.0), axis=0, keepdims=True)
    r2 = jnp.sum(jnp.where(hit2, tot, 0.0), axis=0, keepdims=True)
    base_scr[...] = base_scr[...] + jnp.sum(oh, axis=1, keepdims=True)
    cnt_ref[...] = base_scr[...]

    zero = jnp.zeros((1, tm), I32)
    ert_ref[...] = jnp.concatenate([e1, e2, r1.astype(I32), r2.astype(I32), zero, zero, zero, zero], axis=0)
    wb0_ref[...] = jnp.transpose(jnp.broadcast_to(w1, (LANES, tm)))
    wb1_ref[...] = jnp.transpose(jnp.broadcast_to(w2, (LANES, tm)))


def _router(x, rwhl, rb, tri, tm):
    t = x.shape[0]
    return pl.pallas_call(
        functools.partial(_router_body, tm=tm),
        grid=(t // tm,),
        in_specs=[pl.BlockSpec((tm, D_MODEL), lambda i: (i, 0)),
                  _resident((D_MODEL, 2 * LANES)), _resident((1, LANES)), _resident((tm, tm))],
        out_specs=[pl.BlockSpec((SUBLANES, tm), lambda i: (0, i)),
                   pl.BlockSpec((tm, LANES), lambda i: (i, 0)),
                   pl.BlockSpec((tm, LANES), lambda i: (i, 0)),
                   pl.BlockSpec((N_EXPERTS, LANES), lambda i: (0, 0))],
        out_shape=[jax.ShapeDtypeStruct((SUBLANES, t), I32),
                   jax.ShapeDtypeStruct((t, LANES), F32),
                   jax.ShapeDtypeStruct((t, LANES), F32),
                   jax.ShapeDtypeStruct((N_EXPERTS, LANES), F32)],
        scratch_shapes=[pltpu.VMEM((N_EXPERTS, LANES), F32)],
        compiler_params=_cparams(("arbitrary",)),
        name="moe_router",
    )(x, rwhl, rb, tri)


def _row_copy(src, s, dst, d, sem):
    return pltpu.make_async_copy(src.at[pl.ds(s, 1), :], dst.at[pl.ds(d, 1), :], sem)


def _rows_wait(hbm, n, sem):
    pltpu.make_async_copy(hbm.at[pl.ds(0, n), :], hbm.at[pl.ds(0, n), :], sem).wait()


DISPATCH_SLOTS = 3


def _dispatch_body(pad_lo_ref, pad_hi_ref, p0_ref, p1_ref, x_hbm, xs_out, xbuf, zrow, in_sem, out_sem, zsem,
                   *, tm, n_t):
    i = pl.program_id(0)
    groups = tm // SUBLANES

    def tile_in(tile):
        g0 = pl.multiple_of(tile * groups, groups)
        slot = tile % DISPATCH_SLOTS
        return pltpu.make_async_copy(x_hbm.at[pl.ds(g0, groups)], xbuf.at[slot], in_sem.at[slot])

    @pl.when(i == 0)
    def _():
        tile_in(0).start()
        if n_t > 1:
            tile_in(1).start()
        zrow[...] = jnp.zeros_like(zrow)
        for e in range(N_EXPERTS):
            def put(r, c):
                _row_copy(zrow, 0, xs_out, r, zsem).start()
                return c

            lax.fori_loop(pad_lo_ref[e], pad_hi_ref[e], put, 0)
        for e in range(N_EXPERTS):
            def got(r, c):
                _row_copy(zrow, 0, xs_out, 0, zsem).wait()
                return c

            lax.fori_loop(pad_lo_ref[e], pad_hi_ref[e], got, 0)

    tile_in(i).wait()
    src_tile = xbuf.at[i % DISPATCH_SLOTS]
    sem = out_sem.at[i % 2]

    def issue(g, c):
        t0 = g * SUBLANES
        for s in range(SUBLANES):
            for k, p_ref in enumerate((p0_ref, p1_ref)):
                _row_copy(src_tile.at[g], s, xs_out, p_ref[t0 + s], sem).start(priority=k)
        return c

    lax.fori_loop(0, groups, issue, 0)

    @pl.when(i > 0)
    def _():
        _rows_wait(xs_out, 2 * tm, out_sem.at[(i + 1) % 2])

    @pl.when(i + 2 < n_t)
    def _():
        tile_in(i + 2).start()

    @pl.when(i == n_t - 1)
    def _():
        _rows_wait(xs_out, 2 * tm, sem)


def _dispatch(pad_lo, pad_hi, pos0, pos1, x, n_rows, tm):
    t = x.shape[0]
    n_t = t // tm
    idx = pl.BlockSpec((tm,), lambda i, lo, hi: (i,), memory_space=pltpu.SMEM)
    return pl.pallas_call(
        functools.partial(_dispatch_body, tm=tm, n_t=n_t),
        grid_spec=pltpu.PrefetchScalarGridSpec(
            num_scalar_prefetch=2,
            grid=(n_t,),
            in_specs=[idx, idx, pl.BlockSpec(memory_space=pl.ANY)],
            out_specs=pl.BlockSpec(memory_space=pl.ANY),
            scratch_shapes=[pltpu.VMEM((DISPATCH_SLOTS, tm // SUBLANES, SUBLANES, D_MODEL), F32),
                            pltpu.VMEM((SUBLANES, D_MODEL), F32),
                            pltpu.SemaphoreType.DMA((DISPATCH_SLOTS,)), pltpu.SemaphoreType.DMA((2,)),
                            pltpu.SemaphoreType.DMA(())]),
        out_shape=jax.ShapeDtypeStruct((n_rows, D_MODEL), F32),
        compiler_params=pltpu.CompilerParams(dimension_semantics=("arbitrary",), vmem_limit_bytes=VMEM_LIMIT,
                                             has_side_effects=True, disable_bounds_checks=True),
        name="moe_dispatch",
    )(pad_lo, pad_hi, pos0, pos1, x.reshape(t // SUBLANES, SUBLANES, D_MODEL))


def _ffn_body(te_ref, nu_ref, xs_ref, wgu_ref, wdn_ref, ys_ref):
    used = pl.program_id(0) < nu_ref[0]

    @pl.when(used)
    def _():
        h = _dot(xs_ref[...].astype(BF16), wgu_ref[0, 0])
        hh = (jax.nn.silu(h[:, :D_EXPERT]) * h[:, D_EXPERT:]).astype(BF16)
        ys_ref[...] = _dot(hh, wdn_ref[0, 0])

    @pl.when(jnp.logical_not(used))
    def _():
        ys_ref[...] = jnp.zeros_like(ys_ref)


def _ffn(tile_expert, n_used, xs, w_gu, w_down, layer, tm):
    p = xs.shape[0]
    row_tile = pl.BlockSpec((tm, D_MODEL), lambda i, te, nu: (i, 0))
    return pl.pallas_call(
        _ffn_body,
        grid_spec=pltpu.PrefetchScalarGridSpec(
            num_scalar_prefetch=2,
            grid=(p // tm,),
            in_specs=[row_tile,
                      pl.BlockSpec((1, 1, D_MODEL, 2 * D_EXPERT), lambda i, te, nu: (layer, te[i], 0, 0)),
                      pl.BlockSpec((1, 1, D_EXPERT, D_MODEL), lambda i, te, nu: (layer, te[i], 0, 0))],
            out_specs=row_tile),
        out_shape=jax.ShapeDtypeStruct((p, D_MODEL), F32),
        compiler_params=_cparams(("arbitrary",)),
        name="moe_expert_ffn",
    )(tile_expert, n_used, xs, w_gu, w_down)


def _combine_body(p0_ref, p1_ref, p0n_ref, p1n_ref, ys_hbm, wb0_ref, wb1_ref, x_ref, g_ref, b_ref, *rest,
                  tm, n_t, n_first):
    o_refs = rest[:-6]
    a0, a1, b0, b1, sem_a, sem_b = rest[-6:]
    i = pl.program_id(0)

    def gather(p_refs, bufs, sem):
        def issue(g, c):
            t0 = g * SUBLANES
            for s in range(SUBLANES):
                for k in range(2):
                    _row_copy(ys_hbm, p_refs[k][t0 + s], bufs[k].at[g], s, sem).start(priority=k)
            return c

        lax.fori_loop(0, tm // SUBLANES, issue, 0)

    def reduce(bufs, sem):
        _rows_wait(ys_hbm, 2 * tm, sem)
        reps = D_MODEL // LANES
        w0 = jnp.concatenate([wb0_ref[...]] * reps, axis=1)
        w1 = jnp.concatenate([wb1_ref[...]] * reps, axis=1)
        y0 = bufs[0][...].reshape(tm, D_MODEL)
        y1 = bufs[1][...].reshape(tm, D_MODEL)
        out = _ln(DEEPNORM_ALPHA * x_ref[...] + (w0 * y0 + w1 * y1), g_ref[...], b_ref[...])
        if len(o_refs) == 1:
            o_refs[0][...] = out
        else:
            @pl.when(i < n_first)
            def _():
                o_refs[0][...] = out

            @pl.when(i >= n_first)
            def _():
                o_refs[1][...] = out

    slots = (((a0, a1), sem_a), ((b0, b1), sem_b))

    @pl.when(i == 0)
    def _():
        gather((p0_ref, p1_ref), *slots[0])

    for s in range(2):
        @pl.when(i % 2 == s)
        def _():
            @pl.when(i + 1 < n_t)
            def _():
                gather((p0n_ref, p1n_ref), *slots[1 - s])

            reduce(*slots[s])


def _combine(pos0, pos1, ys, wb0, wb1, x, g, b, tm, split_rows=None):
    t = x.shape[0]
    n_t = t // tm
    out_rows = (t,) if split_rows is None else (split_rows, t - split_rows)
    outs = [jax.ShapeDtypeStruct((r, D_MODEL), F32) for r in out_rows]
    tile = pl.BlockSpec((tm, D_MODEL), lambda i: (i, 0))
    wtile = pl.BlockSpec((tm, LANES), lambda i: (i, 0))
    vec = pl.BlockSpec((1, D_MODEL), lambda i: (0, 0))
    idx = pl.BlockSpec((tm,), lambda i: (i,), memory_space=pltpu.SMEM)
    idx_next = pl.BlockSpec((tm,), lambda i: (jnp.minimum(i + 1, n_t - 1),), memory_space=pltpu.SMEM)
    rows = pltpu.VMEM((tm // SUBLANES, SUBLANES, D_MODEL), F32)
    res = pl.pallas_call(
        functools.partial(_combine_body, tm=tm, n_t=n_t, n_first=out_rows[0] // tm),
        grid=(n_t,),
        in_specs=[idx, idx, idx_next, idx_next, pl.BlockSpec(memory_space=pl.ANY), wtile, wtile, tile, vec, vec],
        out_specs=_tok_specs(outs, tm),
        scratch_shapes=[rows, rows, rows, rows, pltpu.SemaphoreType.DMA(()), pltpu.SemaphoreType.DMA(())],
        out_shape=outs,
        compiler_params=pltpu.CompilerParams(dimension_semantics=("arbitrary",), vmem_limit_bytes=VMEM_LIMIT,
                                             disable_bounds_checks=True),
        name="moe_combine",
    )(pos0, pos1, pos0, pos1, ys, wb0, wb1, x, g, b)
    return res[0] if split_rows is None else tuple(res)


def _moe_layer(x, rwhl, rb, tri, w_gu, w_down, layer, g, b, tm, tme, split_rows=None):
    t = x.shape[0]
    n_tiles = (2 * t) // tme + N_EXPERTS
    ert, wb0, wb1, cnt = _router(x, rwhl, rb, tri, tm)
    counts = cnt[:, 0].astype(I32)
    padded = ((counts + tme - 1) // tme) * tme
    ends = jnp.cumsum(padded).astype(I32)
    off = ends - padded
    n_used = ends[-1:] // tme
    tile_start = jnp.arange(n_tiles, dtype=I32) * tme
    tile_expert = jnp.minimum(jnp.sum(tile_start[:, None] >= ends[None, :], axis=1), N_EXPERTS - 1).astype(I32)
    is_e = ert[0:2, :, None] == jnp.arange(N_EXPERTS, dtype=I32)
    pos = jnp.sum(jnp.where(is_e, off, 0), axis=-1) + ert[2:4]
    pad_hi = ends.at[N_EXPERTS - 1].set(n_tiles * tme)
    xs = _dispatch(off + counts, pad_hi, pos[0], pos[1], x, n_tiles * tme, tm)
    ys = _ffn(tile_expert, n_used, xs, w_gu, w_down, layer, tme)
    return _combine(pos[0], pos[1], ys, wb0, wb1, x, g, b, tm, split_rows)


def _trunk(xs, seq_lens, tm, tme, tt, ln_g, ln_b, a_w_in, a_vn_g, a_vn_b, a_w_s, a_b_s, a_w_out,
           b_w_in, b_conv_w, b_conv_b, b_w_gates, b_b_gates, b_lambda, b_w_out,
           router_w, router_b, moe_w_gu, moe_w_down):
    row = lambda v: v.reshape(1, -1).astype(F32)
    rw = jnp.pad(router_w.astype(F32), ((0, 0), (0, LANES - N_EXPERTS)))
    rwh = rw.astype(BF16)
    rwhl = jnp.concatenate([rwh, (rw - rwh.astype(F32)).astype(BF16)], axis=1)
    rb = jnp.pad(router_b.astype(F32), (0, LANES - N_EXPERTS)).reshape(1, LANES)
    tri = (jnp.arange(tm)[:, None] < jnp.arange(tm)[None, :]).astype(BF16)
    w_gu, w_down = moe_w_gu.astype(BF16), moe_w_down.astype(BF16)
    ia = ib = 0
    x = None
    for layer in range(DEPTH):
        xin = xs if x is None else (x,)
        last = layer == DEPTH - 1
        g0, b0 = row(ln_g[layer, 0]), row(ln_b[layer, 0])
        if layer % 2 == 0:
            bs_full = jnp.repeat(a_b_s[ia].T.astype(F32), CHUNK, axis=1)
            y = _a1(xin, a_w_in[ia].astype(BF16), row(a_vn_g[ia]), row(a_vn_b[ia]), a_w_s[ia].astype(BF16), bs_full, tm)
            x = _out_proj(_op_a_body, [y], a_w_out[ia].astype(BF16), xin, g0, b0, tm, "op_a")
            ia += 1
        else:
            gg, xc = _b1(x, b_w_in[ib].astype(BF16), b_conv_w[ib].astype(F32), row(b_conv_b[ib]), seq_lens, tm)
            wgt = (0.5 * b_w_gates[ib]).astype(BF16)
            wg = jnp.stack([jnp.concatenate([wgt[0], wgt[1]], axis=-1),
                            jnp.concatenate([wgt[2], wgt[3]], axis=-1)])
            hf, hb = _b2(xc, wg, 0.5 * b_b_gates[ib].astype(F32), b_lambda[ib].astype(F32), seq_lens, tt)
            x = _out_proj(_op_b_body, [hf, hb, gg], b_w_out[ib].astype(BF16), (x,), g0, b0, tm, "op_b")
            ib += 1
        x = _moe_layer(x, rwhl, rb, tri, w_gu, w_down, layer,
                       row(ln_g[layer, 1]), row(ln_b[layer, 1]), tm, tme,
                       split_rows=xs[0].shape[0] if last else None)
    return x


def kernel(x_prompt, x_sample, ln_g, ln_b, a_w_in, a_vn_g, a_vn_b, a_w_s, a_b_s, a_w_out, b_w_in, b_conv_w, b_conv_b,
           b_w_gates, b_b_gates, b_lambda, b_w_out, router_w, router_b, moe_w_gu, moe_w_down):
    d = x_prompt.shape[-1]
    seq_lens = (x_prompt.shape[1],) * x_prompt.shape[0] + (x_sample.shape[1],) * x_sample.shape[0]
    xs = (x_prompt.reshape(-1, d), x_sample.reshape(-1, d))
    y_p, y_s = _trunk(xs, seq_lens, TM, TME, TT, ln_g, ln_b, a_w_in, a_vn_g, a_vn_b, a_w_s, a_b_s, a_w_out,
                      b_w_in, b_conv_w, b_conv_b, b_w_gates, b_b_gates, b_lambda, b_w_out,
                      router_w, router_b, moe_w_gu, moe_w_down)
    return y_p.reshape(x_prompt.shape), y_s.reshape(x_sample.shape)
```

```python
import functools

import jax
import jax.numpy as jnp
from jax import lax
from jax.experimental import pallas as pl
from jax.experimental.pallas import tpu as pltpu

F32 = jnp.float32
BF16 = jnp.bfloat16
I32 = jnp.int32

D_MODEL = 2048
DEPTH = 4
CHUNK = 128
A_HALF = D_MODEL
A_GROUPS = 16
RNN_WIDTH = D_MODEL
RNN_BLOCKS = 16
RNN_BDIM = RNN_WIDTH // RNN_BLOCKS
LRU_C = 8.0
N_EXPERTS = 16
N_GROUPS = 4
EXPERTS_PER_GROUP = N_EXPERTS // N_GROUPS
D_EXPERT = D_MODEL // 2
DEEPNORM_ALPHA = (2.0 * DEPTH) ** 0.25
LN_EPS = 1e-5

LANES = 128
SUBLANES = 8
VMEM_LIMIT = 56 * 1024 * 1024

TM = 512
TME = 512
TT = 256


def _cparams(sem):
    return pltpu.CompilerParams(dimension_semantics=sem, vmem_limit_bytes=VMEM_LIMIT)


def _resident(shape):
    nd = len(shape)
    return pl.BlockSpec(shape, lambda *_: (0,) * nd, pipeline_mode=pl.Buffered(1))


def _dot(a, b):
    return jnp.dot(a, b, preferred_element_type=F32)


def _ln(x, g, b):
    mu = jnp.mean(x, axis=-1, keepdims=True)
    xc = x - mu
    var = jnp.mean(xc * xc, axis=-1, keepdims=True)
    return xc * lax.rsqrt(var + LN_EPS) * g + b


def _tok_specs(pieces, tm):
    if len(pieces) == 1:
        return [pl.BlockSpec((tm, D_MODEL), lambda i: (i, 0))]
    n_first = pieces[0].shape[0] // tm
    return [pl.BlockSpec((tm, D_MODEL), lambda i: (jnp.minimum(i, n_first - 1), 0)),
            pl.BlockSpec((tm, D_MODEL), lambda i: (jnp.maximum(i - n_first, 0), 0))]


def _tok_tile(refs, n_first):
    if len(refs) == 1:
        return refs[0][...]
    return jnp.where(pl.program_id(0) < n_first, refs[0][...], refs[1][...])


def _a1_body(*refs, tm, n_x, n_first):
    w_ref, vg_ref, vb_ref, ws_ref, bs_ref, y_ref, s_scr = refs[n_x:]
    xb = _tok_tile(refs[:n_x], n_first).astype(BF16)
    v = jax.nn.gelu(_dot(xb, w_ref[:, A_HALF:]))
    v = _ln(v, vg_ref[...], vb_ref[...]).astype(BF16)
    for c in range(tm // CHUNK):
        rows = slice(c * CHUNK, (c + 1) * CHUNK)
        for g in range(A_GROUPS):
            cols = slice(g * LANES, (g + 1) * LANES)
            s_scr[rows, cols] = _dot(ws_ref[g], v[rows, cols]) + bs_ref[:, cols]
    u = jax.nn.gelu(_dot(xb, w_ref[:, :A_HALF]))
    y_ref[...] = (u * s_scr[...]).astype(BF16)


def _a1(xs, w_in, vn_g, vn_b, w_s, bs_full, tm):
    t = sum(x.shape[0] for x in xs)
    return pl.pallas_call(
        functools.partial(_a1_body, tm=tm, n_x=len(xs), n_first=xs[0].shape[0] // tm),
        grid=(t // tm,),
        in_specs=_tok_specs(xs, tm) + [
            _resident((D_MODEL, 2 * A_HALF)),
            _resident((1, A_HALF)),
            _resident((1, A_HALF)),
            _resident((A_GROUPS, CHUNK, CHUNK)),
            _resident((CHUNK, A_HALF)),
        ],
        out_specs=pl.BlockSpec((tm, A_HALF), lambda i: (i, 0)),
        out_shape=jax.ShapeDtypeStruct((t, A_HALF), BF16),
        scratch_shapes=[pltpu.VMEM((tm, A_HALF), F32)],
        compiler_params=_cparams(("arbitrary",)),
        name="a1_gmlp_front",
    )(*xs, w_in, vn_g, vn_b, w_s, bs_full)


def _half_rows(tm):
    return [slice(0, tm // 2), slice(tm // 2, tm)]


def _op_a_body(*refs, tm, n_x, n_first):
    y_ref, w_ref = refs[:2]
    g_ref, b_ref, o_ref = refs[2 + n_x:]
    x = _tok_tile(refs[2:2 + n_x], n_first)
    for rows in _half_rows(tm):
        m = _dot(y_ref[rows, :], w_ref[...])
        o_ref[rows, :] = _ln(DEEPNORM_ALPHA * x[rows, :] + m, g_ref[...], b_ref[...])


def _op_b_body(*refs, tm, n_x, n_first):
    hf_ref, hb_ref, gg_ref, w_ref = refs[:4]
    g_ref, b_ref, o_ref = refs[4 + n_x:]
    x = _tok_tile(refs[4:4 + n_x], n_first)
    for rows in _half_rows(tm):
        y = ((hf_ref[rows, :].astype(F32) + hb_ref[rows, :].astype(F32)) * gg_ref[rows, :].astype(F32)).astype(BF16)
        m = _dot(y, w_ref[...])
        o_ref[rows, :] = _ln(DEEPNORM_ALPHA * x[rows, :] + m, g_ref[...], b_ref[...])


def _out_proj(body, acts, w_out, xs, g, b, tm, name):
    t = sum(x.shape[0] for x in xs)
    tile = pl.BlockSpec((tm, D_MODEL), lambda i: (i, 0))
    return pl.pallas_call(
        functools.partial(body, tm=tm, n_x=len(xs), n_first=xs[0].shape[0] // tm),
        grid=(t // tm,),
        in_specs=([tile] * len(acts) + [_resident(w_out.shape)] + _tok_specs(xs, tm)
                  + [_resident((1, D_MODEL)), _resident((1, D_MODEL))]),
        out_specs=tile,
        out_shape=jax.ShapeDtypeStruct((t, D_MODEL), F32),
        compiler_params=_cparams(("arbitrary",)),
        name=name,
    )(*acts, w_out, *xs, g, b)


def _edge_hits(idx, marks):
    r = idx == marks[0]
    for m in marks[1:]:
        r = jnp.logical_or(r, idx == m)
    return r


def _seq_edge_tiles(seq_lens, tile):
    bounds = [0]
    for s in seq_lens:
        bounds.append(bounds[-1] + s)
    return tuple(b // tile for b in bounds[:-1]), tuple(b // tile - 1 for b in bounds[1:])


def _b1_body(x_ref, xp_ref, xn_ref, w_ref, cw_ref, cb_ref, gg_ref, xc_ref, *, tm, starts, ends):
    i = pl.program_id(0)
    prev = jnp.where(_edge_hits(i, starts), 0.0, xp_ref[...])
    nxt = jnp.where(_edge_hits(i, ends), 0.0, xn_ref[...])
    xe = jnp.concatenate([prev, x_ref[...], nxt], axis=0).astype(BF16)
    rows = slice(SUBLANES, SUBLANES + tm)
    n = tm + 2 * SUBLANES
    width = 512
    for q in range(RNN_WIDTH // width):
        cols = slice(q * width, (q + 1) * width)
        gg_ref[:, cols] = jax.nn.gelu(_dot(xe, w_ref[:, q * width:(q + 1) * width])[rows, :]).astype(BF16)
        xr = _dot(xe, w_ref[:, RNN_WIDTH + q * width:RNN_WIDTH + (q + 1) * width])
        xc_ref[:, cols] = (cb_ref[:, cols] + cw_ref[0:1, cols] * pltpu.roll(xr, 1, axis=0)[rows, :]
                           + cw_ref[1:2, cols] * xr[rows, :]
                           + cw_ref[2:3, cols] * pltpu.roll(xr, n - 1, axis=0)[rows, :]
                           + cw_ref[3:4, cols] * pltpu.roll(xr, n - 2, axis=0)[rows, :])


def _b1(x, w_in, conv_w, conv_b, seq_lens, tm):
    t = x.shape[0]
    hp = tm // SUBLANES
    n_h = t // SUBLANES
    starts, ends = _seq_edge_tiles(seq_lens, tm)
    tile = pl.BlockSpec((tm, D_MODEL), lambda i: (i, 0))
    prev = pl.BlockSpec((SUBLANES, D_MODEL), lambda i: (jnp.maximum(i * hp - 1, 0), 0))
    nxt = pl.BlockSpec((SUBLANES, D_MODEL), lambda i: (jnp.minimum((i + 1) * hp, n_h - 1), 0))
    return pl.pallas_call(
        functools.partial(_b1_body, tm=tm, starts=starts, ends=ends),
        grid=(t // tm,),
        in_specs=[tile, prev, nxt, _resident((D_MODEL, 2 * RNN_WIDTH)), _resident((4, RNN_WIDTH)), _resident((1, RNN_WIDTH))],
        out_specs=[tile, tile],
        out_shape=[jax.ShapeDtypeStruct((t, RNN_WIDTH), BF16), jax.ShapeDtypeStruct((t, RNN_WIDTH), F32)],
        compiler_params=_cparams(("arbitrary",)),
        name="b1_rglru_front",
    )(x, x, x, w_in, conv_w, conv_b)


def _scan_rows(a_s, b_s, o_ref, carry_scr, *, tt, reverse):
    n_grp = tt // SUBLANES
    width = 512
    row_id = lax.broadcasted_iota(I32, (SUBLANES, width), 0)

    def step(k, carry):
        g = (n_grp - 1 - k) if reverse else k
        r0 = pl.multiple_of(g * SUBLANES, SUBLANES)
        new = []
        for q in range(RNN_WIDTH // width):
            cols = slice(q * width, (q + 1) * width)
            a = a_s[pl.ds(r0, SUBLANES), cols]
            b = b_s[pl.ds(r0, SUBLANES), cols]
            first = row_id == (SUBLANES - 1 if reverse else 0)
            b = jnp.where(first, a * carry[q] + b, b)
            for d in (1, 2, 4):
                if reverse:
                    keep = row_id < SUBLANES - d
                    sh = SUBLANES - d
                else:
                    keep = row_id >= d
                    sh = d
                b = a * jnp.where(keep, pltpu.roll(b, sh, axis=0), 0.0) + b
                if d < 4:
                    a = a * jnp.where(keep, pltpu.roll(a, sh, axis=0), 1.0)
            h = b
            o_ref[pl.ds(r0, SUBLANES), cols] = h.astype(o_ref.dtype)
            edge = h[0:1, :] if reverse else h[SUBLANES - 1:SUBLANES, :]
            new.append(jnp.broadcast_to(edge, (SUBLANES, width)))
        return tuple(new)

    init = tuple(carry_scr[:, q * width:(q + 1) * width] for q in range(RNN_WIDTH // width))
    fin = lax.fori_loop(0, n_grp, step, init)
    for q in range(RNN_WIDTH // width):
        carry_scr[:, q * width:(q + 1) * width] = fin[q]


def _b2_direction(d, x_ref, reset, wg_ref, bg_ref, sp_ref, o_ref, carry_scr, a_s, b_s, *, tt):
    for h in range(RNN_BLOCKS):
        cols = slice(h * RNN_BDIM, (h + 1) * RNN_BDIM)
        xh = x_ref[:, cols]
        g2 = _dot(xh.astype(BF16), wg_ref[d, h])
        t_r = jnp.tanh(g2[:, :RNN_BDIM] + bg_ref[2 * d:2 * d + 1, cols])
        t_i = jnp.tanh(g2[:, RNN_BDIM:] + bg_ref[2 * d + 1:2 * d + 2, cols])
        c2 = sp_ref[d:d + 1, cols]
        log_a = c2 * t_r + c2
        a = jnp.exp(log_a)
        z = jnp.tanh(log_a) * (-1.0 - a * a)
        mult = jnp.where(z > 0.0, z * lax.rsqrt(z), 0.0)
        a_s[:, cols] = a
        b_s[:, cols] = mult * ((t_i + 1.0) * xh)

    @pl.when(reset)
    def _():
        carry_scr[...] = jnp.zeros_like(carry_scr)

    _scan_rows(a_s, b_s, o_ref, carry_scr, tt=tt, reverse=(d == 1))


def _b2_body(xf_ref, xb_ref, wg_ref, bg_ref, lam_ref, hf_ref, hb_ref, cf_scr, cbk_scr, sp_scr, a_s, b_s,
             *, tt, n_t, starts, ends):
    i = pl.program_id(0)
    j = n_t - 1 - i
    sp_scr[...] = (-0.5 * LRU_C) * jax.nn.softplus(-lam_ref[...])
    common = dict(wg_ref=wg_ref, bg_ref=bg_ref, sp_ref=sp_scr, a_s=a_s, b_s=b_s, tt=tt)
    _b2_direction(0, xf_ref, _edge_hits(i, starts), o_ref=hf_ref, carry_scr=cf_scr, **common)
    _b2_direction(1, xb_ref, _edge_hits(j, ends), o_ref=hb_ref, carry_scr=cbk_scr, **common)


def _b2(xc, wg, b_gates, lam, seq_lens, tt):
    t = xc.shape[0]
    n_t = t // tt
    starts, ends = _seq_edge_tiles(seq_lens, tt)
    w = RNN_WIDTH
    main_f = pl.BlockSpec((tt, w), lambda i: (i, 0))
    main_b = pl.BlockSpec((tt, w), lambda i: (n_t - 1 - i, 0))
    return pl.pallas_call(
        functools.partial(_b2_body, tt=tt, n_t=n_t, starts=starts, ends=ends),
        grid=(n_t,),
        in_specs=[main_f, main_b, _resident(wg.shape), _resident((4, w)), _resident((2, w))],
        out_specs=[main_f, main_b],
        out_shape=[jax.ShapeDtypeStruct((t, w), BF16), jax.ShapeDtypeStruct((t, w), BF16)],
        scratch_shapes=[pltpu.VMEM((SUBLANES, w), F32), pltpu.VMEM((SUBLANES, w), F32), pltpu.VMEM((2, w), F32),
                        pltpu.VMEM((tt, w), F32), pltpu.VMEM((tt, w), F32)],
        compiler_params=_cparams(("arbitrary",)),
        name="b2_gates_scan",
    )(xc, xc, wg, b_gates, lam)


def _top2_of4(a):
    m1 = jnp.maximum(jnp.maximum(a[0], a[1]), jnp.maximum(a[2], a[3]))
    i1 = jnp.where(a[0] == m1, 0, jnp.where(a[1] == m1, 1, jnp.where(a[2] == m1, 2, 3)))
    b = [jnp.where(i1 == k, -1.0, a[k]) for k in range(4)]
    m2 = jnp.maximum(jnp.maximum(b[0], b[1]), jnp.maximum(b[2], b[3]))
    i2 = jnp.where(b[0] == m2, 0, jnp.where(b[1] == m2, 1, jnp.where(b[2] == m2, 2, 3)))
    return m1, i1, m2, i2


def _router_body(x_ref, whl_ref, rb_ref, tri_ref, ert_ref, wb0_ref, wb1_ref, cnt_ref, base_scr, *, tm):
    @pl.when(pl.program_id(0) == 0)
    def _():
        base_scr[...] = jnp.zeros_like(base_scr)

    x = x_ref[...]
    xh = x.astype(BF16)
    xl = (x - xh.astype(F32)).astype(BF16)
    z = _dot(xh, whl_ref[...])
    logits = z[:, :LANES] + z[:, LANES:] + _dot(xl, whl_ref[:, :LANES]) + rb_ref[...]
    lt = jnp.transpose(logits)
    l = [lt[e:e + 1, :] for e in range(N_EXPERTS)]
    mx = l[0]
    for e in range(1, N_EXPERTS):
        mx = jnp.maximum(mx, l[e])
    ex = [jnp.exp(v - mx) for v in l]
    den = ex[0]
    for e in range(1, N_EXPERTS):
        den = den + ex[e]
    p = [v / den for v in ex]

    tops = [_top2_of4(p[g * EXPERTS_PER_GROUP:(g + 1) * EXPERTS_PER_GROUP]) for g in range(N_GROUPS)]
    score = [t[0] + t[2] for t in tops]
    best = jnp.maximum(jnp.maximum(score[0], score[1]), jnp.maximum(score[2], score[3]))
    gsel = jnp.where(score[0] == best, 0, jnp.where(score[1] == best, 1, jnp.where(score[2] == best, 2, 3)))

    def pick(field):
        return jnp.where(gsel == 0, tops[0][field],
                         jnp.where(gsel == 1, tops[1][field], jnp.where(gsel == 2, tops[2][field], tops[3][field])))

    v1, i1, v2, i2 = pick(0), pick(1), pick(2), pick(3)
    e1 = gsel * EXPERTS_PER_GROUP + i1
    e2 = gsel * EXPERTS_PER_GROUP + i2
    vs = v1 + v2
    w1 = v1 / vs
    w2 = v2 / vs

    eid = lax.broadcasted_iota(I32, (N_EXPERTS, tm), 0)
    hit1 = eid == e1
    hit2 = eid == e2
    oh = jnp.where(hit1, 1.0, 0.0) + jnp.where(hit2, 1.0, 0.0)
    before = _dot(oh.astype(BF16), tri_ref[...])
    tot = before + jnp.concatenate([base_scr[...]] * (tm // LANES), axis=1)
    r1 = jnp.sum(jnp.where(hit1, tot, 0.0), axis=0, keepdims=True)
    r2 = jnp.sum(jnp.where(hit2, tot, 0.0), axis=0, keepdims=True)
    base_scr[...] = base_scr[...] + jnp.sum(oh, axis=1, keepdims=True)
    cnt_ref[...] = base_scr[...]

    zero = jnp.zeros((1, tm), I32)
    ert_ref[...] = jnp.concatenate([e1, e2, r1.astype(I32), r2.astype(I32), zero, zero, zero, zero], axis=0)
    wb0_ref[...] = jnp.transpose(jnp.broadcast_to(w1, (LANES, tm)))
    wb1_ref[...] = jnp.transpose(jnp.broadcast_to(w2, (LANES, tm)))


def _router(x, rwhl, rb, tri, tm):
    t = x.shape[0]
    return pl.pallas_call(
        functools.partial(_router_body, tm=tm),
        grid=(t // tm,),
        in_specs=[pl.BlockSpec((tm, D_MODEL), lambda i: (i, 0)),
                  _resident((D_MODEL, 2 * LANES)), _resident((1, LANES)), _resident((tm, tm))],
        out_specs=[pl.BlockSpec((SUBLANES, tm), lambda i: (0, i)),
                   pl.BlockSpec((tm, LANES), lambda i: (i, 0)),
                   pl.BlockSpec((tm, LANES), lambda i: (i, 0)),
                   pl.BlockSpec((N_EXPERTS, LANES), lambda i: (0, 0))],
        out_shape=[jax.ShapeDtypeStruct((SUBLANES, t), I32),
                   jax.ShapeDtypeStruct((t, LANES), F32),
                   jax.ShapeDtypeStruct((t, LANES), F32),
                   jax.ShapeDtypeStruct((N_EXPERTS, LANES), F32)],
        scratch_shapes=[pltpu.VMEM((N_EXPERTS, LANES), F32)],
        compiler_params=_cparams(("arbitrary",)),
        name="moe_router",
    )(x, rwhl, rb, tri)


def _row_copy(src, s, dst, d, sem):
    return pltpu.make_async_copy(src.at[pl.ds(s, 1), :], dst.at[pl.ds(d, 1), :], sem)


def _rows_wait(hbm, n, sem):
    pltpu.make_async_copy(hbm.at[pl.ds(0, n), :], hbm.at[pl.ds(0, n), :], sem).wait()


DISPATCH_SLOTS = 3


def _dispatch_body(pad_lo_ref, pad_hi_ref, p0_ref, p1_ref, x_hbm, xs_out, xbuf, zrow, in_sem, out_sem, zsem,
                   *, tm, n_t):
    i = pl.program_id(0)
    groups = tm // SUBLANES

    def tile_in(tile):
        g0 = pl.multiple_of(tile * groups, groups)
        slot = tile % DISPATCH_SLOTS
        return pltpu.make_async_copy(x_hbm.at[pl.ds(g0, groups)], xbuf.at[slot], in_sem.at[slot])

    @pl.when(i == 0)
    def _():
        tile_in(0).start()
        if n_t > 1:
            tile_in(1).start()
        zrow[...] = jnp.zeros_like(zrow)
        for e in range(N_EXPERTS):
            def put(r, c):
                _row_copy(zrow, 0, xs_out, r, zsem).start()
                return c

            lax.fori_loop(pad_lo_ref[e], pad_hi_ref[e], put, 0)
        for e in range(N_EXPERTS):
            def got(r, c):
                _row_copy(zrow, 0, xs_out, 0, zsem).wait()
                return c

            lax.fori_loop(pad_lo_ref[e], pad_hi_ref[e], got, 0)

    tile_in(i).wait()
    src_tile = xbuf.at[i % DISPATCH_SLOTS]
    sem = out_sem.at[i % 2]

    def issue(g, c):
        t0 = g * SUBLANES
        for s in range(SUBLANES):
            for k, p_ref in enumerate((p0_ref, p1_ref)):
                _row_copy(src_tile.at[g], s, xs_out, p_ref[t0 + s], sem).start(priority=k)
        return c

    lax.fori_loop(0, groups, issue, 0)

    @pl.when(i > 0)
    def _():
        _rows_wait(xs_out, 2 * tm, out_sem.at[(i + 1) % 2])

    @pl.when(i + 2 < n_t)
    def _():
        tile_in(i + 2).start()

    @pl.when(i == n_t - 1)
    def _():
        _rows_wait(xs_out, 2 * tm, sem)


def _dispatch(pad_lo, pad_hi, pos0, pos1, x, n_rows, tm):
    t = x.shape[0]
    n_t = t // tm
    idx = pl.BlockSpec((tm,), lambda i, lo, hi: (i,), memory_space=pltpu.SMEM)
    return pl.pallas_call(
        functools.partial(_dispatch_body, tm=tm, n_t=n_t),
        grid_spec=pltpu.PrefetchScalarGridSpec(
            num_scalar_prefetch=2,
            grid=(n_t,),
            in_specs=[idx, idx, pl.BlockSpec(memory_space=pl.ANY)],
            out_specs=pl.BlockSpec(memory_space=pl.ANY),
            scratch_shapes=[pltpu.VMEM((DISPATCH_SLOTS, tm // SUBLANES, SUBLANES, D_MODEL), F32),
                            pltpu.VMEM((SUBLANES, D_MODEL), F32),
                            pltpu.SemaphoreType.DMA((DISPATCH_SLOTS,)), pltpu.SemaphoreType.DMA((2,)),
                            pltpu.SemaphoreType.DMA(())]),
        out_shape=jax.ShapeDtypeStruct((n_rows, D_MODEL), F32),
        compiler_params=pltpu.CompilerParams(dimension_semantics=("arbitrary",), vmem_limit_bytes=VMEM_LIMIT,
                                             has_side_effects=True, disable_bounds_checks=True),
        name="moe_dispatch",
    )(pad_lo, pad_hi, pos0, pos1, x.reshape(t // SUBLANES, SUBLANES, D_MODEL))


def _ffn_body(te_ref, nu_ref, xs_ref, wgu_ref, wdn_ref, ys_ref):
    used = pl.program_id(0) < nu_ref[0]

    @pl.when(used)
    def _():
        h = _dot(xs_ref[...].astype(BF16), wgu_ref[0, 0])
        hh = (jax.nn.silu(h[:, :D_EXPERT]) * h[:, D_EXPERT:]).astype(BF16)
        ys_ref[...] = _dot(hh, wdn_ref[0, 0])

    @pl.when(jnp.logical_not(used))
    def _():
        ys_ref[...] = jnp.zeros_like(ys_ref)


def _ffn(tile_expert, n_used, xs, w_gu, w_down, layer, tm):
    p = xs.shape[0]
    row_tile = pl.BlockSpec((tm, D_MODEL), lambda i, te, nu: (i, 0))
    return pl.pallas_call(
        _ffn_body,
        grid_spec=pltpu.PrefetchScalarGridSpec(
            num_scalar_prefetch=2,
            grid=(p // tm,),
            in_specs=[row_tile,
                      pl.BlockSpec((1, 1, D_MODEL, 2 * D_EXPERT), lambda i, te, nu: (layer, te[i], 0, 0)),
                      pl.BlockSpec((1, 1, D_EXPERT, D_MODEL), lambda i, te, nu: (layer, te[i], 0, 0))],
            out_specs=row_tile),
        out_shape=jax.ShapeDtypeStruct((p, D_MODEL), F32),
        compiler_params=_cparams(("arbitrary",)),
        name="moe_expert_ffn",
    )(tile_expert, n_used, xs, w_gu, w_down)


def _combine_body(p0_ref, p1_ref, p0n_ref, p1n_ref, ys_hbm, wb0_ref, wb1_ref, x_ref, g_ref, b_ref, *rest,
                  tm, n_t, n_first):
    o_refs = rest[:-6]
    a0, a1, b0, b1, sem_a, sem_b = rest[-6:]
    i = pl.program_id(0)

    def gather(p_refs, bufs, sem):
        def issue(g, c):
            t0 = g * SUBLANES
            for s in range(SUBLANES):
                for k in range(2):
                    _row_copy(ys_hbm, p_refs[k][t0 + s], bufs[k].at[g], s, sem).start(priority=k)
            return c

        lax.fori_loop(0, tm // SUBLANES, issue, 0)

    def reduce(bufs, sem):
        _rows_wait(ys_hbm, 2 * tm, sem)
        reps = D_MODEL // LANES
        w0 = jnp.concatenate([wb0_ref[...]] * reps, axis=1)
        w1 = jnp.concatenate([wb1_ref[...]] * reps, axis=1)
        y0 = bufs[0][...].reshape(tm, D_MODEL)
        y1 = bufs[1][...].reshape(tm, D_MODEL)
        out = _ln(DEEPNORM_ALPHA * x_ref[...] + (w0 * y0 + w1 * y1), g_ref[...], b_ref[...])
        if len(o_refs) == 1:
            o_refs[0][...] = out
        else:
            @pl.when(i < n_first)
            def _():
                o_refs[0][...] = out

            @pl.when(i >= n_first)
            def _():
                o_refs[1][...] = out

    slots = (((a0, a1), sem_a), ((b0, b1), sem_b))

    @pl.when(i == 0)
    def _():
        gather((p0_ref, p1_ref), *slots[0])

    for s in range(2):
        @pl.when(i % 2 == s)
        def _():
            @pl.when(i + 1 < n_t)
            def _():
                gather((p0n_ref, p1n_ref), *slots[1 - s])

            reduce(*slots[s])


def _combine(pos0, pos1, ys, wb0, wb1, x, g, b, tm, split_rows=None):
    t = x.shape[0]
    n_t = t // tm
    out_rows = (t,) if split_rows is None else (split_rows, t - split_rows)
    outs = [jax.ShapeDtypeStruct((r, D_MODEL), F32) for r in out_rows]
    tile = pl.BlockSpec((tm, D_MODEL), lambda i: (i, 0))
    wtile = pl.BlockSpec((tm, LANES), lambda i: (i, 0))
    vec = pl.BlockSpec((1, D_MODEL), lambda i: (0, 0))
    idx = pl.BlockSpec((tm,), lambda i: (i,), memory_space=pltpu.SMEM)
    idx_next = pl.BlockSpec((tm,), lambda i: (jnp.minimum(i + 1, n_t - 1),), memory_space=pltpu.SMEM)
    rows = pltpu.VMEM((tm // SUBLANES, SUBLANES, D_MODEL), F32)
    res = pl.pallas_call(
        functools.partial(_combine_body, tm=tm, n_t=n_t, n_first=out_rows[0] // tm),
        grid=(n_t,),
        in_specs=[idx, idx, idx_next, idx_next, pl.BlockSpec(memory_space=pl.ANY), wtile, wtile, tile, vec, vec],
        out_specs=_tok_specs(outs, tm),
        scratch_shapes=[rows, rows, rows, rows, pltpu.SemaphoreType.DMA(()), pltpu.SemaphoreType.DMA(())],
        out_shape=outs,
        compiler_params=pltpu.CompilerParams(dimension_semantics=("arbitrary",), vmem_limit_bytes=VMEM_LIMIT,
                                             disable_bounds_checks=True),
        name="moe_combine",
    )(pos0, pos1, pos0, pos1, ys, wb0, wb1, x, g, b)
    return res[0] if split_rows is None else tuple(res)


def _moe_layer(x, rwhl, rb, tri, w_gu, w_down, layer, g, b, tm, tme, split_rows=None):
    t = x.shape[0]
    n_tiles = (2 * t) // tme + N_EXPERTS
    ert, wb0, wb1, cnt = _router(x, rwhl, rb, tri, tm)
    counts = cnt[:, 0].astype(I32)
    padded = ((counts + tme - 1) // tme) * tme
    ends = jnp.cumsum(padded).astype(I32)
    off = ends - padded
    n_used = ends[-1:] // tme
    tile_start = jnp.arange(n_tiles, dtype=I32) * tme
    tile_expert = jnp.minimum(jnp.sum(tile_start[:, None] >= ends[None, :], axis=1), N_EXPERTS - 1).astype(I32)
    is_e = ert[0:2, :, None] == jnp.arange(N_EXPERTS, dtype=I32)
    pos = jnp.sum(jnp.where(is_e, off, 0), axis=-1) + ert[2:4]
    pad_hi = ends.at[N_EXPERTS - 1].set(n_tiles * tme)
    xs = _dispatch(off + counts, pad_hi, pos[0], pos[1], x, n_tiles * tme, tm)
    ys = _ffn(tile_expert, n_used, xs, w_gu, w_down, layer, tme)
    return _combine(pos[0], pos[1], ys, wb0, wb1, x, g, b, tm, split_rows)


def _trunk(xs, seq_lens, tm, tme, tt, ln_g, ln_b, a_w_in, a_vn_g, a_vn_b, a_w_s, a_b_s, a_w_out,
           b_w_in, b_conv_w, b_conv_b, b_w_gates, b_b_gates, b_lambda, b_w_out,
           router_w, router_b, moe_w_gu, moe_w_down):
    row = lambda v: v.reshape(1, -1).astype(F32)
    rw = jnp.pad(router_w.astype(F32), ((0, 0), (0, LANES - N_EXPERTS)))
    rwh = rw.astype(BF16)
    rwhl = jnp.concatenate([rwh, (rw - rwh.astype(F32)).astype(BF16)], axis=1)
    rb = jnp.pad(router_b.astype(F32), (0, LANES - N_EXPERTS)).reshape(1, LANES)
    tri = (jnp.arange(tm)[:, None] < jnp.arange(tm)[None, :]).astype(BF16)
    w_gu, w_down = moe_w_gu.astype(BF16), moe_w_down.astype(BF16)
    ia = ib = 0
    x = None
    for layer in range(DEPTH):
        xin = xs if x is None else (x,)
        last = layer == DEPTH - 1
        g0, b0 = row(ln_g[layer, 0]), row(ln_b[layer, 0])
        if layer % 2 == 0:
            bs_full = jnp.repeat(a_b_s[ia].T.astype(F32), CHUNK, axis=1)
            y = _a1(xin, a_w_in[ia].astype(BF16), row(a_vn_g[ia]), row(a_vn_b[ia]), a_w_s[ia].astype(BF16), bs_full, tm)
            x = _out_proj(_op_a_body, [y], a_w_out[ia].astype(BF16), xin, g0, b0, tm, "op_a")
            ia += 1
        else:
            gg, xc = _b1(x, b_w_in[ib].astype(BF16), 0.5 * b_conv_w[ib].astype(F32), 0.5 * row(b_conv_b[ib]), seq_lens, tm)
            wgt = b_w_gates[ib].astype(BF16)
            wg = jnp.stack([jnp.concatenate([wgt[0], wgt[1]], axis=-1),
                            jnp.concatenate([wgt[2], wgt[3]], axis=-1)])
            hf, hb = _b2(xc, wg, 0.5 * b_b_gates[ib].astype(F32), b_lambda[ib].astype(F32), seq_lens, tt)
            x = _out_proj(_op_b_body, [hf, hb, gg], b_w_out[ib].astype(BF16), (x,), g0, b0, tm, "op_b")
            ib += 1
        x = _moe_layer(x, rwhl, rb, tri, w_gu, w_down, layer,
                       row(ln_g[layer, 1]), row(ln_b[layer, 1]), tm, tme,
                       split_rows=xs[0].shape[0] if last else None)
    return x


def kernel(x_prompt, x_sample, ln_g, ln_b, a_w_in, a_vn_g, a_vn_b, a_w_s, a_b_s, a_w_out, b_w_in, b_conv_w, b_conv_b,
           b_w_gates, b_b_gates, b_lambda, b_w_out, router_w, router_b, moe_w_gu, moe_w_down):
    d = x_prompt.shape[-1]
    seq_lens = (x_prompt.shape[1],) * x_prompt.shape[0] + (x_sample.shape[1],) * x_sample.shape[0]
    xs = (x_prompt.reshape(-1, d), x_sample.reshape(-1, d))
    y_p, y_s = _trunk(xs, seq_lens, TM, TME, TT, ln_g, ln_b, a_w_in, a_vn_g, a_vn_b, a_w_s, a_b_s, a_w_out,
                      b_w_in, b_conv_w, b_conv_b, b_w_gates, b_b_gates, b_lambda, b_w_out,
                      router_w, router_b, moe_w_gu, moe_w_down)
    return y_p.reshape(x_prompt.shape), y_s.reshape(x_sample.shape)
```

```python
import functools

import jax
import jax.numpy as jnp
from jax import lax
from jax.experimental import pallas as pl
from jax.experimental.pallas import tpu as pltpu

F32 = jnp.float32
BF16 = jnp.bfloat16
I32 = jnp.int32

D_MODEL = 2048
DEPTH = 4
CHUNK = 128
A_HALF = D_MODEL
A_GROUPS = 16
RNN_WIDTH = D_MODEL
RNN_BLOCKS = 16
RNN_BDIM = RNN_WIDTH // RNN_BLOCKS
LRU_C = 8.0
N_EXPERTS = 16
N_GROUPS = 4
EXPERTS_PER_GROUP = N_EXPERTS // N_GROUPS
D_EXPERT = D_MODEL // 2
DEEPNORM_ALPHA = (2.0 * DEPTH) ** 0.25
LN_EPS = 1e-5

LANES = 128
SUBLANES = 8
VMEM_LIMIT = 56 * 1024 * 1024

TM = 512
TME = 512
TT = 256


def _cparams(sem):
    return pltpu.CompilerParams(dimension_semantics=sem, vmem_limit_bytes=VMEM_LIMIT)


def _resident(shape):
    nd = len(shape)
    return pl.BlockSpec(shape, lambda *_: (0,) * nd, pipeline_mode=pl.Buffered(1))


def _dot(a, b):
    return jnp.dot(a, b, preferred_element_type=F32)


def _ln(x, g, b):
    mu = jnp.mean(x, axis=-1, keepdims=True)
    xc = x - mu
    var = jnp.mean(xc * xc, axis=-1, keepdims=True)
    return xc * lax.rsqrt(var + LN_EPS) * g + b


def _tok_specs(pieces, tm):
    if len(pieces) == 1:
        return [pl.BlockSpec((tm, D_MODEL), lambda i: (i, 0))]
    n_first = pieces[0].shape[0] // tm
    return [pl.BlockSpec((tm, D_MODEL), lambda i: (jnp.minimum(i, n_first - 1), 0)),
            pl.BlockSpec((tm, D_MODEL), lambda i: (jnp.maximum(i - n_first, 0), 0))]


def _tok_tile(refs, n_first):
    if len(refs) == 1:
        return refs[0][...]
    return jnp.where(pl.program_id(0) < n_first, refs[0][...], refs[1][...])


def _a1_body(*refs, tm, n_x, n_first):
    w_ref, vg_ref, vb_ref, ws_ref, bs_ref, y_ref, s_scr = refs[n_x:]
    xb = _tok_tile(refs[:n_x], n_first).astype(BF16)
    v = jax.nn.gelu(_dot(xb, w_ref[:, A_HALF:]))
    v = _ln(v, vg_ref[...], vb_ref[...]).astype(BF16)
    for c in range(tm // CHUNK):
        rows = slice(c * CHUNK, (c + 1) * CHUNK)
        for g in range(A_GROUPS):
            cols = slice(g * LANES, (g + 1) * LANES)
            s_scr[rows, cols] = _dot(ws_ref[g], v[rows, cols]) + bs_ref[:, cols]
    u = jax.nn.gelu(_dot(xb, w_ref[:, :A_HALF]))
    y_ref[...] = (u * s_scr[...]).astype(BF16)


def _a1(xs, w_in, vn_g, vn_b, w_s, bs_full, tm):
    t = sum(x.shape[0] for x in xs)
    return pl.pallas_call(
        functools.partial(_a1_body, tm=tm, n_x=len(xs), n_first=xs[0].shape[0] // tm),
        grid=(t // tm,),
        in_specs=_tok_specs(xs, tm) + [
            _resident((D_MODEL, 2 * A_HALF)),
            _resident((1, A_HALF)),
            _resident((1, A_HALF)),
            _resident((A_GROUPS, CHUNK, CHUNK)),
            _resident((CHUNK, A_HALF)),
        ],
        out_specs=pl.BlockSpec((tm, A_HALF), lambda i: (i, 0)),
        out_shape=jax.ShapeDtypeStruct((t, A_HALF), BF16),
        scratch_shapes=[pltpu.VMEM((tm, A_HALF), F32)],
        compiler_params=_cparams(("arbitrary",)),
        name="a1_gmlp_front",
    )(*xs, w_in, vn_g, vn_b, w_s, bs_full)


def _half_rows(tm):
    return [slice(0, tm // 2), slice(tm // 2, tm)]


def _op_tail(refs, tm):
    g_ref, b_ref, whl_ref, rb_ref, tri_ref, o_ref = refs[:6]
    return g_ref, b_ref, o_ref, lambda: _route(o_ref[...], whl_ref, rb_ref, tri_ref, *refs[6:], tm)


def _op_a_body(*refs, tm, n_x, n_first):
    y_ref, w_ref = refs[:2]
    g_ref, b_ref, o_ref, route = _op_tail(refs[2 + n_x:], tm)
    x = _tok_tile(refs[2:2 + n_x], n_first)
    for rows in _half_rows(tm):
        m = _dot(y_ref[rows, :], w_ref[...])
        o_ref[rows, :] = _ln(DEEPNORM_ALPHA * x[rows, :] + m, g_ref[...], b_ref[...])
    route()


def _op_b_body(*refs, tm, n_x, n_first):
    hf_ref, hb_ref, gg_ref, w_ref = refs[:4]
    g_ref, b_ref, o_ref, route = _op_tail(refs[4 + n_x:], tm)
    x = _tok_tile(refs[4:4 + n_x], n_first)
    for rows in _half_rows(tm):
        y = ((hf_ref[rows, :].astype(F32) + hb_ref[rows, :].astype(F32)) * gg_ref[rows, :].astype(F32)).astype(BF16)
        m = _dot(y, w_ref[...])
        o_ref[rows, :] = _ln(DEEPNORM_ALPHA * x[rows, :] + m, g_ref[...], b_ref[...])
    route()


def _out_proj(body, acts, w_out, xs, g, b, rwhl, rb, tri, tm, name):
    t = sum(x.shape[0] for x in xs)
    tile = pl.BlockSpec((tm, D_MODEL), lambda i: (i, 0))
    return pl.pallas_call(
        functools.partial(body, tm=tm, n_x=len(xs), n_first=xs[0].shape[0] // tm),
        grid=(t // tm,),
        in_specs=([tile] * len(acts) + [_resident(w_out.shape)] + _tok_specs(xs, tm)
                  + [_resident((1, D_MODEL)), _resident((1, D_MODEL)),
                     _resident((D_MODEL, 2 * LANES)), _resident((1, LANES)), _resident((tm, tm))]),
        out_specs=[tile,
                   pl.BlockSpec((SUBLANES, tm), lambda i: (0, i)),
                   pl.BlockSpec((tm, LANES), lambda i: (i, 0)),
                   pl.BlockSpec((tm, LANES), lambda i: (i, 0)),
                   pl.BlockSpec((N_EXPERTS, LANES), lambda i: (0, 0))],
        out_shape=[jax.ShapeDtypeStruct((t, D_MODEL), F32),
                   jax.ShapeDtypeStruct((SUBLANES, t), I32),
                   jax.ShapeDtypeStruct((t, LANES), F32),
                   jax.ShapeDtypeStruct((t, LANES), F32),
                   jax.ShapeDtypeStruct((N_EXPERTS, LANES), F32)],
        scratch_shapes=[pltpu.VMEM((N_EXPERTS, LANES), F32)],
        compiler_params=_cparams(("arbitrary",)),
        name=name,
    )(*acts, w_out, *xs, g, b, rwhl, rb, tri)


def _edge_hits(idx, marks):
    r = idx == marks[0]
    for m in marks[1:]:
        r = jnp.logical_or(r, idx == m)
    return r


def _seq_edge_tiles(seq_lens, tile):
    bounds = [0]
    for s in seq_lens:
        bounds.append(bounds[-1] + s)
    return tuple(b // tile for b in bounds[:-1]), tuple(b // tile - 1 for b in bounds[1:])


def _b1_body(x_ref, xp_ref, xn_ref, w_ref, cw_ref, cb_ref, gg_ref, xc_ref, *, tm, starts, ends):
    i = pl.program_id(0)
    prev = jnp.where(_edge_hits(i, starts), 0.0, xp_ref[...])
    nxt = jnp.where(_edge_hits(i, ends), 0.0, xn_ref[...])
    xe = jnp.concatenate([prev, x_ref[...], nxt], axis=0).astype(BF16)
    rows = slice(SUBLANES, SUBLANES + tm)
    n = tm + 2 * SUBLANES
    width = 512
    for q in range(RNN_WIDTH // width):
        cols = slice(q * width, (q + 1) * width)
        gg_ref[:, cols] = jax.nn.gelu(_dot(xe, w_ref[:, q * width:(q + 1) * width])[rows, :]).astype(BF16)
        xr = _dot(xe, w_ref[:, RNN_WIDTH + q * width:RNN_WIDTH + (q + 1) * width])
        xc_ref[:, cols] = (cb_ref[:, cols] + cw_ref[0:1, cols] * pltpu.roll(xr, 1, axis=0)[rows, :]
                           + cw_ref[1:2, cols] * xr[rows, :]
                           + cw_ref[2:3, cols] * pltpu.roll(xr, n - 1, axis=0)[rows, :]
                           + cw_ref[3:4, cols] * pltpu.roll(xr, n - 2, axis=0)[rows, :])


def _b1(x, w_in, conv_w, conv_b, seq_lens, tm):
    t = x.shape[0]
    hp = tm // SUBLANES
    n_h = t // SUBLANES
    starts, ends = _seq_edge_tiles(seq_lens, tm)
    tile = pl.BlockSpec((tm, D_MODEL), lambda i: (i, 0))
    prev = pl.BlockSpec((SUBLANES, D_MODEL), lambda i: (jnp.maximum(i * hp - 1, 0), 0))
    nxt = pl.BlockSpec((SUBLANES, D_MODEL), lambda i: (jnp.minimum((i + 1) * hp, n_h - 1), 0))
    return pl.pallas_call(
        functools.partial(_b1_body, tm=tm, starts=starts, ends=ends),
        grid=(t // tm,),
        in_specs=[tile, prev, nxt, _resident((D_MODEL, 2 * RNN_WIDTH)), _resident((4, RNN_WIDTH)), _resident((1, RNN_WIDTH))],
        out_specs=[tile, tile],
        out_shape=[jax.ShapeDtypeStruct((t, RNN_WIDTH), BF16), jax.ShapeDtypeStruct((t, RNN_WIDTH), F32)],
        compiler_params=_cparams(("arbitrary",)),
        name="b1_rglru_front",
    )(x, x, x, w_in, conv_w, conv_b)


def _scan_rows(a_s, b_s, o_ref, carry_scr, *, tt, reverse):
    n_grp = tt // SUBLANES
    width = 512
    row_id = lax.broadcasted_iota(I32, (SUBLANES, width), 0)

    def step(k, carry):
        g = (n_grp - 1 - k) if reverse else k
        r0 = pl.multiple_of(g * SUBLANES, SUBLANES)
        new = []
        for q in range(RNN_WIDTH // width):
            cols = slice(q * width, (q + 1) * width)
            a = a_s[pl.ds(r0, SUBLANES), cols]
            b = b_s[pl.ds(r0, SUBLANES), cols]
            first = row_id == (SUBLANES - 1 if reverse else 0)
            b = jnp.where(first, a * carry[q] + b, b)
            for d in (1, 2, 4):
                if reverse:
                    keep = row_id < SUBLANES - d
                    sh = SUBLANES - d
                else:
                    keep = row_id >= d
                    sh = d
                b = a * jnp.where(keep, pltpu.roll(b, sh, axis=0), 0.0) + b
                if d < 4:
                    a = a * jnp.where(keep, pltpu.roll(a, sh, axis=0), 1.0)
            h = b
            o_ref[pl.ds(r0, SUBLANES), cols] = h.astype(o_ref.dtype)
            edge = h[0:1, :] if reverse else h[SUBLANES - 1:SUBLANES, :]
            new.append(jnp.broadcast_to(edge, (SUBLANES, width)))
        return tuple(new)

    init = tuple(carry_scr[:, q * width:(q + 1) * width] for q in range(RNN_WIDTH // width))
    fin = lax.fori_loop(0, n_grp, step, init)
    for q in range(RNN_WIDTH // width):
        carry_scr[:, q * width:(q + 1) * width] = fin[q]


def _b2_direction(d, x_ref, reset, wg_ref, bg_ref, sp_ref, o_ref, carry_scr, a_s, b_s, *, tt):
    for h in range(RNN_BLOCKS):
        cols = slice(h * RNN_BDIM, (h + 1) * RNN_BDIM)
        xh = x_ref[:, cols]
        g2 = _dot(xh.astype(BF16), wg_ref[d, h])
        t_r = jnp.tanh(g2[:, :RNN_BDIM] + bg_ref[2 * d:2 * d + 1, cols])
        t_i = jnp.tanh(g2[:, RNN_BDIM:] + bg_ref[2 * d + 1:2 * d + 2, cols])
        c2 = sp_ref[d:d + 1, cols]
        log_a = c2 * t_r + c2
        a = jnp.exp(log_a)
        z = jnp.tanh(log_a) * (-1.0 - a * a)
        mult = jnp.where(z > 0.0, z * lax.rsqrt(z), 0.0)
        a_s[:, cols] = a
        b_s[:, cols] = mult * ((t_i + 1.0) * xh)

    @pl.when(reset)
    def _():
        carry_scr[...] = jnp.zeros_like(carry_scr)

    _scan_rows(a_s, b_s, o_ref, carry_scr, tt=tt, reverse=(d == 1))


def _b2_body(xf_ref, xb_ref, wg_ref, bg_ref, lam_ref, hf_ref, hb_ref, cf_scr, cbk_scr, sp_scr, a_s, b_s,
             *, tt, n_t, starts, ends):
    i = pl.program_id(0)
    j = n_t - 1 - i
    sp_scr[...] = (-0.5 * LRU_C) * jax.nn.softplus(-lam_ref[...])
    common = dict(wg_ref=wg_ref, bg_ref=bg_ref, sp_ref=sp_scr, a_s=a_s, b_s=b_s, tt=tt)
    _b2_direction(0, xf_ref, _edge_hits(i, starts), o_ref=hf_ref, carry_scr=cf_scr, **common)
    _b2_direction(1, xb_ref, _edge_hits(j, ends), o_ref=hb_ref, carry_scr=cbk_scr, **common)


def _b2(xc, wg, b_gates, lam, seq_lens, tt):
    t = xc.shape[0]
    n_t = t // tt
    starts, ends = _seq_edge_tiles(seq_lens, tt)
    w = RNN_WIDTH
    main_f = pl.BlockSpec((tt, w), lambda i: (i, 0))
    main_b = pl.BlockSpec((tt, w), lambda i: (n_t - 1 - i, 0))
    return pl.pallas_call(
        functools.partial(_b2_body, tt=tt, n_t=n_t, starts=starts, ends=ends),
        grid=(n_t,),
        in_specs=[main_f, main_b, _resident(wg.shape), _resident((4, w)), _resident((2, w))],
        out_specs=[main_f, main_b],
        out_shape=[jax.ShapeDtypeStruct((t, w), BF16), jax.ShapeDtypeStruct((t, w), BF16)],
        scratch_shapes=[pltpu.VMEM((SUBLANES, w), F32), pltpu.VMEM((SUBLANES, w), F32), pltpu.VMEM((2, w), F32),
                        pltpu.VMEM((tt, w), F32), pltpu.VMEM((tt, w), F32)],
        compiler_params=_cparams(("arbitrary",)),
        name="b2_gates_scan",
    )(xc, xc, wg, b_gates, lam)


def _top2_of4(a):
    m1 = jnp.maximum(jnp.maximum(a[0], a[1]), jnp.maximum(a[2], a[3]))
    i1 = jnp.where(a[0] == m1, 0, jnp.where(a[1] == m1, 1, jnp.where(a[2] == m1, 2, 3)))
    b = [jnp.where(i1 == k, -1.0, a[k]) for k in range(4)]
    m2 = jnp.maximum(jnp.maximum(b[0], b[1]), jnp.maximum(b[2], b[3]))
    i2 = jnp.where(b[0] == m2, 0, jnp.where(b[1] == m2, 1, jnp.where(b[2] == m2, 2, 3)))
    return m1, i1, m2, i2


def _route(x, whl_ref, rb_ref, tri_ref, ert_ref, wb0_ref, wb1_ref, cnt_ref, base_scr, tm):
    @pl.when(pl.program_id(0) == 0)
    def _():
        base_scr[...] = jnp.zeros_like(base_scr)

    xh = x.astype(BF16)
    xl = (x - xh.astype(F32)).astype(BF16)
    z = _dot(xh, whl_ref[...])
    logits = z[:, :LANES] + z[:, LANES:] + _dot(xl, whl_ref[:, :LANES]) + rb_ref[...]
    lt = jnp.transpose(logits)
    l = [lt[e:e + 1, :] for e in range(N_EXPERTS)]
    mx = l[0]
    for e in range(1, N_EXPERTS):
        mx = jnp.maximum(mx, l[e])
    ex = [jnp.exp(v - mx) for v in l]
    den = ex[0]
    for e in range(1, N_EXPERTS):
        den = den + ex[e]
    p = [v / den for v in ex]

    tops = [_top2_of4(p[g * EXPERTS_PER_GROUP:(g + 1) * EXPERTS_PER_GROUP]) for g in range(N_GROUPS)]
    score = [t[0] + t[2] for t in tops]
    best = jnp.maximum(jnp.maximum(score[0], score[1]), jnp.maximum(score[2], score[3]))
    gsel = jnp.where(score[0] == best, 0, jnp.where(score[1] == best, 1, jnp.where(score[2] == best, 2, 3)))

    def pick(field):
        return jnp.where(gsel == 0, tops[0][field],
                         jnp.where(gsel == 1, tops[1][field], jnp.where(gsel == 2, tops[2][field], tops[3][field])))

    v1, i1, v2, i2 = pick(0), pick(1), pick(2), pick(3)
    e1 = gsel * EXPERTS_PER_GROUP + i1
    e2 = gsel * EXPERTS_PER_GROUP + i2
    vs = v1 + v2
    w1 = v1 / vs
    w2 = v2 / vs

    eid = lax.broadcasted_iota(I32, (N_EXPERTS, tm), 0)
    hit1 = eid == e1
    hit2 = eid == e2
    oh = jnp.where(hit1, 1.0, 0.0) + jnp.where(hit2, 1.0, 0.0)
    before = _dot(oh.astype(BF16), tri_ref[...])
    tot = before + jnp.concatenate([base_scr[...]] * (tm // LANES), axis=1)
    r1 = jnp.sum(jnp.where(hit1, tot, 0.0), axis=0, keepdims=True)
    r2 = jnp.sum(jnp.where(hit2, tot, 0.0), axis=0, keepdims=True)
    base_scr[...] = base_scr[...] + jnp.sum(oh, axis=1, keepdims=True)
    cnt_ref[...] = base_scr[...]

    zero = jnp.zeros((1, tm), I32)
    ert_ref[...] = jnp.concatenate([e1, e2, r1.astype(I32), r2.astype(I32), zero, zero, zero, zero], axis=0)
    wb0_ref[...] = jnp.transpose(jnp.broadcast_to(w1, (LANES, tm)))
    wb1_ref[...] = jnp.transpose(jnp.broadcast_to(w2, (LANES, tm)))


def _row_copy(src, s, dst, d, sem):
    return pltpu.make_async_copy(src.at[pl.ds(s, 1), :], dst.at[pl.ds(d, 1), :], sem)


def _rows_wait(hbm, n, sem):
    pltpu.make_async_copy(hbm.at[pl.ds(0, n), :], hbm.at[pl.ds(0, n), :], sem).wait()


DISPATCH_SLOTS = 3


def _dispatch_body(pad_lo_ref, pad_hi_ref, p0_ref, p1_ref, x_hbm, xs_out, xbuf, zrow, in_sem, out_sem, zsem,
                   *, tm, n_t):
    i = pl.program_id(0)
    groups = tm // SUBLANES

    def tile_in(tile):
        g0 = pl.multiple_of(tile * groups, groups)
        slot = tile % DISPATCH_SLOTS
        return pltpu.make_async_copy(x_hbm.at[pl.ds(g0, groups)], xbuf.at[slot], in_sem.at[slot])

    @pl.when(i == 0)
    def _():
        tile_in(0).start()
        if n_t > 1:
            tile_in(1).start()
        zrow[...] = jnp.zeros_like(zrow)
        for e in range(N_EXPERTS):
            def put(r, c):
                _row_copy(zrow, 0, xs_out, r, zsem).start()
                return c

            lax.fori_loop(pad_lo_ref[e], pad_hi_ref[e], put, 0)
        for e in range(N_EXPERTS):
            def got(r, c):
                _row_copy(zrow, 0, xs_out, 0, zsem).wait()
                return c

            lax.fori_loop(pad_lo_ref[e], pad_hi_ref[e], got, 0)

    tile_in(i).wait()
    src_tile = xbuf.at[i % DISPATCH_SLOTS]
    sem = out_sem.at[i % 2]

    def issue(g, c):
        t0 = g * SUBLANES
        for s in range(SUBLANES):
            for k, p_ref in enumerate((p0_ref, p1_ref)):
                _row_copy(src_tile.at[g], s, xs_out, p_ref[t0 + s], sem).start(priority=k)
        return c

    lax.fori_loop(0, groups, issue, 0)

    @pl.when(i > 0)
    def _():
        _rows_wait(xs_out, 2 * tm, out_sem.at[(i + 1) % 2])

    @pl.when(i + 2 < n_t)
    def _():
        tile_in(i + 2).start()

    @pl.when(i == n_t - 1)
    def _():
        _rows_wait(xs_out, 2 * tm, sem)


def _dispatch(pad_lo, pad_hi, pos0, pos1, x, n_rows, tm):
    t = x.shape[0]
    n_t = t // tm
    idx = pl.BlockSpec((tm,), lambda i, lo, hi: (i,), memory_space=pltpu.SMEM)
    return pl.pallas_call(
        functools.partial(_dispatch_body, tm=tm, n_t=n_t),
        grid_spec=pltpu.PrefetchScalarGridSpec(
            num_scalar_prefetch=2,
            grid=(n_t,),
            in_specs=[idx, idx, pl.BlockSpec(memory_space=pl.ANY)],
            out_specs=pl.BlockSpec(memory_space=pl.ANY),
            scratch_shapes=[pltpu.VMEM((DISPATCH_SLOTS, tm // SUBLANES, SUBLANES, D_MODEL), F32),
                            pltpu.VMEM((SUBLANES, D_MODEL), F32),
                            pltpu.SemaphoreType.DMA((DISPATCH_SLOTS,)), pltpu.SemaphoreType.DMA((2,)),
                            pltpu.SemaphoreType.DMA(())]),
        out_shape=jax.ShapeDtypeStruct((n_rows, D_MODEL), F32),
        compiler_params=pltpu.CompilerParams(dimension_semantics=("arbitrary",), vmem_limit_bytes=VMEM_LIMIT,
                                             has_side_effects=True, disable_bounds_checks=True),
        name="moe_dispatch",
    )(pad_lo, pad_hi, pos0, pos1, x.reshape(t // SUBLANES, SUBLANES, D_MODEL))


def _ffn_body(te_ref, nu_ref, xs_ref, wgu_ref, wdn_ref, ys_ref):
    used = pl.program_id(0) < nu_ref[0]

    @pl.when(used)
    def _():
        h = _dot(xs_ref[...].astype(BF16), wgu_ref[0, 0])
        hh = (jax.nn.silu(h[:, :D_EXPERT]) * h[:, D_EXPERT:]).astype(BF16)
        ys_ref[...] = _dot(hh, wdn_ref[0, 0])

    @pl.when(jnp.logical_not(used))
    def _():
        ys_ref[...] = jnp.zeros_like(ys_ref)


def _ffn(tile_expert, n_used, xs, w_gu, w_down, layer, tm):
    p = xs.shape[0]
    row_tile = pl.BlockSpec((tm, D_MODEL), lambda i, te, nu: (i, 0))
    return pl.pallas_call(
        _ffn_body,
        grid_spec=pltpu.PrefetchScalarGridSpec(
            num_scalar_prefetch=2,
            grid=(p // tm,),
            in_specs=[row_tile,
                      pl.BlockSpec((1, 1, D_MODEL, 2 * D_EXPERT), lambda i, te, nu: (layer, te[i], 0, 0)),
                      pl.BlockSpec((1, 1, D_EXPERT, D_MODEL), lambda i, te, nu: (layer, te[i], 0, 0))],
            out_specs=row_tile),
        out_shape=jax.ShapeDtypeStruct((p, D_MODEL), F32),
        compiler_params=_cparams(("arbitrary",)),
        name="moe_expert_ffn",
    )(tile_expert, n_used, xs, w_gu, w_down)


def _combine_body(p0_ref, p1_ref, p0n_ref, p1n_ref, ys_hbm, wb0_ref, wb1_ref, x_ref, g_ref, b_ref, *rest,
                  tm, n_t, n_first):
    o_refs = rest[:-6]
    a0, a1, b0, b1, sem_a, sem_b = rest[-6:]
    i = pl.program_id(0)

    def gather(p_refs, bufs, sem):
        def issue(g, c):
            t0 = g * SUBLANES
            for s in range(SUBLANES):
                for k in range(2):
                    _row_copy(ys_hbm, p_refs[k][t0 + s], bufs[k].at[g], s, sem).start(priority=k)
            return c

        lax.fori_loop(0, tm // SUBLANES, issue, 0)

    def reduce(bufs, sem):
        _rows_wait(ys_hbm, 2 * tm, sem)
        reps = D_MODEL // LANES
        w0 = jnp.concatenate([wb0_ref[...]] * reps, axis=1)
        w1 = jnp.concatenate([wb1_ref[...]] * reps, axis=1)
        y0 = bufs[0][...].reshape(tm, D_MODEL)
        y1 = bufs[1][...].reshape(tm, D_MODEL)
        out = _ln(DEEPNORM_ALPHA * x_ref[...] + (w0 * y0 + w1 * y1), g_ref[...], b_ref[...])
        if len(o_refs) == 1:
            o_refs[0][...] = out
        else:
            @pl.when(i < n_first)
            def _():
                o_refs[0][...] = out

            @pl.when(i >= n_first)
            def _():
                o_refs[1][...] = out

    slots = (((a0, a1), sem_a), ((b0, b1), sem_b))

    @pl.when(i == 0)
    def _():
        gather((p0_ref, p1_ref), *slots[0])

    for s in range(2):
        @pl.when(i % 2 == s)
        def _():
            @pl.when(i + 1 < n_t)
            def _():
                gather((p0n_ref, p1n_ref), *slots[1 - s])

            reduce(*slots[s])


def _combine(pos0, pos1, ys, wb0, wb1, x, g, b, tm, split_rows=None):
    t = x.shape[0]
    n_t = t // tm
    out_rows = (t,) if split_rows is None else (split_rows, t - split_rows)
    outs = [jax.ShapeDtypeStruct((r, D_MODEL), F32) for r in out_rows]
    tile = pl.BlockSpec((tm, D_MODEL), lambda i: (i, 0))
    wtile = pl.BlockSpec((tm, LANES), lambda i: (i, 0))
    vec = pl.BlockSpec((1, D_MODEL), lambda i: (0, 0))
    idx = pl.BlockSpec((tm,), lambda i: (i,), memory_space=pltpu.SMEM)
    idx_next = pl.BlockSpec((tm,), lambda i: (jnp.minimum(i + 1, n_t - 1),), memory_space=pltpu.SMEM)
    rows = pltpu.VMEM((tm // SUBLANES, SUBLANES, D_MODEL), F32)
    res = pl.pallas_call(
        functools.partial(_combine_body, tm=tm, n_t=n_t, n_first=out_rows[0] // tm),
        grid=(n_t,),
        in_specs=[idx, idx, idx_next, idx_next, pl.BlockSpec(memory_space=pl.ANY), wtile, wtile, tile, vec, vec],
        out_specs=_tok_specs(outs, tm),
        scratch_shapes=[rows, rows, rows, rows, pltpu.SemaphoreType.DMA(()), pltpu.SemaphoreType.DMA(())],
        out_shape=outs,
        compiler_params=pltpu.CompilerParams(dimension_semantics=("arbitrary",), vmem_limit_bytes=VMEM_LIMIT,
                                             disable_bounds_checks=True),
        name="moe_combine",
    )(pos0, pos1, pos0, pos1, ys, wb0, wb1, x, g, b)
    return res[0] if split_rows is None else tuple(res)


def _moe_layer(x, routing, w_gu, w_down, layer, g, b, tm, tme, split_rows=None):
    t = x.shape[0]
    n_tiles = (2 * t) // tme + N_EXPERTS
    ert, wb0, wb1, cnt = routing
    counts = cnt[:, 0].astype(I32)
    padded = ((counts + tme - 1) // tme) * tme
    ends = jnp.cumsum(padded).astype(I32)
    off = ends - padded
    n_used = ends[-1:] // tme
    tile_start = jnp.arange(n_tiles, dtype=I32) * tme
    tile_expert = jnp.minimum(jnp.sum(tile_start[:, None] >= ends[None, :], axis=1), N_EXPERTS - 1).astype(I32)
    is_e = ert[0:2, :, None] == jnp.arange(N_EXPERTS, dtype=I32)
    pos = jnp.sum(jnp.where(is_e, off, 0), axis=-1) + ert[2:4]
    pad_hi = ends.at[N_EXPERTS - 1].set(n_tiles * tme)
    xs = _dispatch(off + counts, pad_hi, pos[0], pos[1], x, n_tiles * tme, tm)
    ys = _ffn(tile_expert, n_used, xs, w_gu, w_down, layer, tme)
    return _combine(pos[0], pos[1], ys, wb0, wb1, x, g, b, tm, split_rows)


def _trunk(xs, seq_lens, tm, tme, tt, ln_g, ln_b, a_w_in, a_vn_g, a_vn_b, a_w_s, a_b_s, a_w_out,
           b_w_in, b_conv_w, b_conv_b, b_w_gates, b_b_gates, b_lambda, b_w_out,
           router_w, router_b, moe_w_gu, moe_w_down):
    row = lambda v: v.reshape(1, -1).astype(F32)
    rw = jnp.pad(router_w.astype(F32), ((0, 0), (0, LANES - N_EXPERTS)))
    rwh = rw.astype(BF16)
    rwhl = jnp.concatenate([rwh, (rw - rwh.astype(F32)).astype(BF16)], axis=1)
    rb = jnp.pad(router_b.astype(F32), (0, LANES - N_EXPERTS)).reshape(1, LANES)
    tri = (jnp.arange(tm)[:, None] < jnp.arange(tm)[None, :]).astype(BF16)
    w_gu, w_down = moe_w_gu.astype(BF16), moe_w_down.astype(BF16)
    ia = ib = 0
    x = None
    for layer in range(DEPTH):
        xin = xs if x is None else (x,)
        last = layer == DEPTH - 1
        g0, b0 = row(ln_g[layer, 0]), row(ln_b[layer, 0])
        if layer % 2 == 0:
            bs_full = jnp.repeat(a_b_s[ia].T.astype(F32), CHUNK, axis=1)
            y = _a1(xin, a_w_in[ia].astype(BF16), row(a_vn_g[ia]), row(a_vn_b[ia]), a_w_s[ia].astype(BF16), bs_full, tm)
            x, *routing = _out_proj(_op_a_body, [y], a_w_out[ia].astype(BF16), xin, g0, b0, rwhl, rb, tri, tm, "op_a")
            ia += 1
        else:
            gg, xc = _b1(x, b_w_in[ib].astype(BF16), 0.5 * b_conv_w[ib].astype(F32), 0.5 * row(b_conv_b[ib]), seq_lens, tm)
            wgt = b_w_gates[ib].astype(BF16)
            wg = jnp.stack([jnp.concatenate([wgt[0], wgt[1]], axis=-1),
                            jnp.concatenate([wgt[2], wgt[3]], axis=-1)])
            hf, hb = _b2(xc, wg, 0.5 * b_b_gates[ib].astype(F32), b_lambda[ib].astype(F32), seq_lens, tt)
            x, *routing = _out_proj(_op_b_body, [hf, hb, gg], b_w_out[ib].astype(BF16), (x,), g0, b0, rwhl, rb, tri, tm, "op_b")
            ib += 1
        x = _moe_layer(x, routing, w_gu, w_down, layer,
                       row(ln_g[layer, 1]), row(ln_b[layer, 1]), tm, tme,
                       split_rows=xs[0].shape[0] if last else None)
    return x


def kernel(x_prompt, x_sample, ln_g, ln_b, a_w_in, a_vn_g, a_vn_b, a_w_s, a_b_s, a_w_out, b_w_in, b_conv_w, b_conv_b,
           b_w_gates, b_b_gates, b_lambda, b_w_out, router_w, router_b, moe_w_gu, moe_w_down):
    d = x_prompt.shape[-1]
    seq_lens = (x_prompt.shape[1],) * x_prompt.shape[0] + (x_sample.shape[1],) * x_sample.shape[0]
    xs = (x_prompt.reshape(-1, d), x_sample.reshape(-1, d))
    y_p, y_s = _trunk(xs, seq_lens, TM, TME, TT, ln_g, ln_b, a_w_in, a_vn_g, a_vn_b, a_w_s, a_b_s, a_w_out,
                      b_w_in, b_conv_w, b_conv_b, b_w_gates, b_b_gates, b_lambda, b_w_out,
                      router_w, router_b, moe_w_gu, moe_w_down)
    return y_p.reshape(x_prompt.shape), y_s.reshape(x_sample.shape)
```
